```python
import math
import jax, jax.numpy as jnp
from jax import lax
import numpy as np

D_MODEL = 1024
BATCH = 32
SEQ = 2048
DEPTH = 1

ML_H = 4
ML_HD = D_MODEL // 8
ML_W = ML_H * ML_HD
ML_CONV = 4
ML_CHUNK = 64
ATT_H = 8
ATT_KV = 2
ATT_HD = 64
ATT_W = ATT_H * ATT_HD
WINDOW = 128
ATT_BLOCK = WINDOW
ROPE_THETA = 10000.0
MIX_W = ML_W + ATT_W
IN_COLS = 4 * ML_W + 2 * ML_H + ATT_W + 2 * ATT_KV * ATT_HD
D_FF = 2816
FFN_CONV = 3
N_MOD = 6
RMS_EPS = 1e-6

kernel_name = "hybrid_mlstm_swa_convffn_layer"


def rms_norm(x, g):
    xf = x.astype(jnp.float32)
    y = xf * lax.rsqrt(jnp.mean(xf * xf, axis=-1, keepdims=True) + RMS_EPS)
    return (y * g.astype(jnp.float32)).astype(x.dtype)


def causal_dwconv(x, w, b):
    k_len, ch = w.shape
    y = lax.conv_general_dilated(
        x, w[:, None, :].astype(x.dtype), window_strides=(1,),
        padding=((k_len - 1, 0),), dimension_numbers=("NWC", "WIO", "NWC"),
        feature_group_count=ch)
    return y + b.astype(x.dtype)


def rope(x, pos):
    half = x.shape[-1] // 2
    inv = ROPE_THETA ** (-jnp.arange(half, dtype=jnp.float32) / half)
    ang = pos.astype(jnp.float32)[:, None] * inv[None, :]
    cos = jnp.cos(ang)[None, :, None, :]
    sin = jnp.sin(ang)[None, :, None, :]
    xf = x.astype(jnp.float32)
    x1, x2 = xf[..., :half], xf[..., half:]
    return jnp.concatenate([x1 * cos - x2 * sin, x2 * cos + x1 * sin], axis=-1)


def sliding_window_attention(q, k, v, sinks):
    B, S, H, D = q.shape
    G = k.shape[2]
    R = H // G
    nb = S // ATT_BLOCK
    qb = q.reshape(B, nb, ATT_BLOCK, G, R, D)
    kb = k.reshape(B, nb, ATT_BLOCK, G, D)
    vb = v.reshape(B, nb, ATT_BLOCK, G, D)
    pad = ((0, 0), (1, 0), (0, 0), (0, 0), (0, 0))
    kband = jnp.concatenate([jnp.pad(kb, pad)[:, :-1], kb], axis=2)
    vband = jnp.concatenate([jnp.pad(vb, pad)[:, :-1], vb], axis=2)
    s = jnp.einsum('bnqgrd,bnkgd->bngrqk', qb, kband) * (1.0 / math.sqrt(D))
    qi = jnp.arange(ATT_BLOCK)[:, None] + ATT_BLOCK
    kj = jnp.arange(2 * ATT_BLOCK)[None, :]
    rel = qi - kj
    band = (rel >= 0) & (rel < WINDOW)
    kpos = jnp.arange(nb)[:, None, None] * ATT_BLOCK + kj[None] - ATT_BLOCK
    valid = band[None] & (kpos >= 0)
    s = jnp.where(valid[None, :, None, None], s, -jnp.inf)
    sk = sinks.astype(jnp.float32).reshape(G, R)[None, None, :, :, None, None]
    m = jnp.maximum(jnp.max(s, axis=-1, keepdims=True), sk)
    p = jnp.exp(s - m)
    denom = jnp.sum(p, axis=-1, keepdims=True) + jnp.exp(sk - m)
    p = p / denom
    o = jnp.einsum('bngrqk,bnkgd->bnqgrd', p, vband)
    return o.reshape(B, S, H, D)


def mlstm_chunkwise(q, k, v, i_pre, logf):
    B, S, H, D = q.shape
    L = ML_CHUNK
    nc = S // L
    to_c = lambda t: t.reshape(B, nc, L, H, D).transpose(1, 0, 3, 2, 4)
    to_cg = lambda t: t.reshape(B, nc, L, H).transpose(1, 0, 3, 2)
    causal = jnp.tril(jnp.ones((L, L), dtype=bool))

    def step(carry, xs):
        C, n, m = carry
        qc, kc, vc, ic, fc = xs
        b = jnp.cumsum(fc, axis=-1)
        dmat = jnp.where(causal, b[..., :, None] - b[..., None, :] + ic[..., None, :], -jnp.inf)
        m_inter = b + m[..., None]
        m_t = jnp.maximum(m_inter, jnp.max(dmat, axis=-1))
        w_inter = jnp.exp(m_inter - m_t)
        sqk = jnp.einsum('bhtd,bhsd->bhts', qc, kc) * jnp.exp(dmat - m_t[..., None])
        num = jnp.einsum('bhts,bhsd->bhtd', sqk, vc) + \
            w_inter[..., None] * jnp.einsum('bhvk,bhtk->bhtv', C, qc)
        den = jnp.sum(sqk, axis=-1) + w_inter * jnp.einsum('bhk,bhtk->bht', n, qc)
        h = num / jnp.maximum(jnp.abs(den), jnp.exp(-m_t))[..., None]
        b_last = b[..., -1]
        g = b_last[..., None] - b + ic
        m_new = jnp.maximum(b_last + m, jnp.max(g, axis=-1))
        decay = jnp.exp(b_last + m - m_new)
        ws = jnp.exp(g - m_new[..., None])
        C_new = decay[..., None, None] * C + jnp.einsum('bhsv,bhsk->bhvk', vc * ws[..., None], kc)
        n_new = decay[..., None] * n + jnp.einsum('bhs,bhsk->bhk', ws, kc)
        return (C_new, n_new, m_new), h

    init = (jnp.zeros((B, H, D, D), jnp.float32), jnp.zeros((B, H, D), jnp.float32),
            jnp.zeros((B, H), jnp.float32))
    _, hs = lax.scan(step, init, (to_c(q), to_c(k), to_c(v), to_cg(i_pre), to_cg(logf)))
    return hs.transpose(1, 0, 3, 2, 4).reshape(B, S, H, D)


def token_mixer(h, w_in, ml_conv_w, ml_conv_b, ml_i_b, ml_f_b, ml_norm_g,
                attn_sinks, attn_norm_g, w_out):
    B, S, _ = h.shape
    proj = h @ w_in
    cuts = [2 * ML_W, 3 * ML_W, 4 * ML_W, 4 * ML_W + 2 * ML_H, 4 * ML_W + 2 * ML_H + ATT_W]
    qk_ml, v_ml, o_ml, gates, q_at, kv_at = jnp.split(proj, cuts, axis=-1)
    qk_ml = jax.nn.silu(causal_dwconv(qk_ml, ml_conv_w, ml_conv_b)).astype(jnp.float32)
    q_ml = qk_ml[..., :ML_W].reshape(B, S, ML_H, ML_HD)
    k_ml = qk_ml[..., ML_W:].reshape(B, S, ML_H, ML_HD) * (1.0 / math.sqrt(ML_HD))
    v_ml = v_ml.astype(jnp.float32).reshape(B, S, ML_H, ML_HD)
    gates = gates.astype(jnp.float32)
    i_pre = gates[..., :ML_H] + ml_i_b.astype(jnp.float32)
    logf = jax.nn.log_sigmoid(gates[..., ML_H:] + ml_f_b.astype(jnp.float32))
    h_ml = mlstm_chunkwise(q_ml, k_ml, v_ml, i_pre, logf)
    h_ml = rms_norm(h_ml, ml_norm_g)
    h_ml = h_ml * jax.nn.sigmoid(o_ml.astype(jnp.float32).reshape(B, S, ML_H, ML_HD))
    h_ml = h_ml.reshape(B, S, ML_W).astype(h.dtype)
    pos = jnp.arange(S)
    q_at = rope(q_at.reshape(B, S, ATT_H, ATT_HD), pos)
    k_at = rope(kv_at[..., :ATT_KV * ATT_HD].reshape(B, S, ATT_KV, ATT_HD), pos)
    v_at = kv_at[..., ATT_KV * ATT_HD:].astype(jnp.float32).reshape(B, S, ATT_KV, ATT_HD)
    o_at = sliding_window_attention(q_at, k_at, v_at, attn_sinks).reshape(B, S, ATT_W)
    o_at = rms_norm(o_at, attn_norm_g).astype(h.dtype)
    return jnp.concatenate([h_ml, o_at], axis=-1) @ w_out


def conv_ffn(h, w_up, conv_w, conv_b, w_down):
    u = causal_dwconv(h @ w_up, conv_w, conv_b)
    g, val = u[..., :D_FF], u[..., D_FF:]
    return (jax.nn.gelu(g, approximate=True) * val) @ w_down


def setup_inputs(seed: int = 0) -> dict:
    key = jax.random.key(seed)
    ks = jax.random.split(key, 24)
    f32 = jnp.float32
    nrm = lambda k, shp, s: jax.random.normal(k, shp, f32) * s
    gain = lambda k, shp: 1.0 + 0.05 * jax.random.normal(k, shp, f32)
    L = DEPTH
    f_bias = jnp.linspace(3.0, 6.0, ML_H, dtype=f32)[None, :] + nrm(ks[7], (L, ML_H), 0.1)
    return {
        "x": nrm(ks[0], (BATCH, SEQ, D_MODEL), 1.0),
        "c": nrm(ks[1], (BATCH, D_MODEL), 1.0),
        "w_ada": nrm(ks[2], (L, D_MODEL, N_MOD * D_MODEL), 0.5 * D_MODEL ** -0.5),
        "b_ada": nrm(ks[3], (L, N_MOD * D_MODEL), 0.02),
        "pre_mix_g": gain(ks[4], (L, D_MODEL)),
        "w_in": nrm(ks[5], (L, D_MODEL, IN_COLS), D_MODEL ** -0.5),
        "ml_conv_w": nrm(ks[6], (L, ML_CONV, 2 * ML_W), ML_CONV ** -0.5),
        "ml_conv_b": nrm(ks[8], (L, 2 * ML_W), 0.02),
        "ml_i_b": nrm(ks[9], (L, ML_H), 0.1),
        "ml_f_b": f_bias,
        "ml_norm_g": gain(ks[10], (L, ML_H, ML_HD)),
        "attn_sinks": nrm(ks[11], (L, ATT_H), 1.0),
        "attn_norm_g": gain(ks[12], (L, ATT_W)),
        "w_out": nrm(ks[13], (L, MIX_W, D_MODEL), MIX_W ** -0.5),
        "post_mix_g": gain(ks[14], (L, D_MODEL)),
        "pre_ffn_g": gain(ks[15], (L, D_MODEL)),
        "w_up": nrm(ks[16], (L, D_MODEL, 2 * D_FF), D_MODEL ** -0.5),
        "ffn_conv_w": nrm(ks[17], (L, FFN_CONV, 2 * D_FF), FFN_CONV ** -0.5),
        "ffn_conv_b": nrm(ks[18], (L, 2 * D_FF), 0.02),
        "w_down": nrm(ks[19], (L, D_FF, D_MODEL), D_FF ** -0.5),
        "post_ffn_g": gain(ks[20], (L, D_MODEL)),
    }


def reference(x, c, w_ada, b_ada, pre_mix_g, w_in, ml_conv_w, ml_conv_b, ml_i_b,
              ml_f_b, ml_norm_g, attn_sinks, attn_norm_g, w_out, post_mix_g,
              pre_ffn_g, w_up, ffn_conv_w, ffn_conv_b, w_down, post_ffn_g):
    c_act = jax.nn.silu(c)
    for l in range(DEPTH):
        mod = c_act @ w_ada[l] + b_ada[l]
        sh1, sc1, g1, sh2, sc2, g2 = [t[:, None, :] for t in jnp.split(mod, N_MOD, axis=-1)]
        h = rms_norm(x, pre_mix_g[l]) * (1.0 + sc1) + sh1
        y = token_mixer(h, w_in[l], ml_conv_w[l], ml_conv_b[l], ml_i_b[l], ml_f_b[l],
                        ml_norm_g[l], attn_sinks[l], attn_norm_g[l], w_out[l])
        x = x + g1 * rms_norm(y, post_mix_g[l])
        h = rms_norm(x, pre_ffn_g[l]) * (1.0 + sc2) + sh2
        y = conv_ffn(h, w_up[l], ffn_conv_w[l], ffn_conv_b[l], w_down[l])
        x = x + g2 * rms_norm(y, post_ffn_g[l])
    return x
```

```python
import functools
import math

import numpy as np
import jax
import jax.numpy as jnp
from jax import lax
from jax.experimental import pallas as pl
from jax.experimental.pallas import tpu as pltpu

F32 = jnp.float32
BF16 = jnp.bfloat16

D_MODEL = 1024
ML_H = 4
ML_HD = 128
ML_W = ML_H * ML_HD
ML_CONV = 4
ATT_H = 8
ATT_KV = 2
ATT_HD = 64
ATT_W = ATT_H * ATT_HD
WINDOW = 128
ROPE_THETA = 10000.0
MIX_W = ML_W + ATT_W
IN_COLS = 4 * ML_W + 2 * ML_H + ATT_W + 2 * ATT_KV * ATT_HD
D_FF = 2816
FFN_CONV = 3
N_MOD = 6
RMS_EPS = 1e-6

LANES = 128
SUBLANES = 8
BLK = 128
NEG = -1e30

C_QK = 0
C_V = 2 * ML_W
C_O = 3 * ML_W
C_G = 4 * ML_W
C_QA = C_G + LANES
C_KA = C_QA + ATT_W
C_VA = C_KA + ATT_KV * LANES
NW = C_VA + ATT_KV * LANES

FF_TILE = 256
N_FF_TILES = D_FF // FF_TILE

VMEM_LIMIT = 56 * 1024 * 1024


def _in_proj_perm():
    zero = IN_COLS
    idx = list(range(0, 4 * ML_W))
    g0 = 4 * ML_W
    idx += [g0 + j for j in range(2 * ML_H)] + [zero] * (LANES - 2 * ML_H)
    q0 = g0 + 2 * ML_H
    half = ATT_HD // 2
    for p in range(ATT_H // 2):
        for l in range(LANES):
            quarter, i = divmod(l, half)
            head = 2 * p + (quarter % 2)
            d = i + half * (quarter // 2)
            idx.append(q0 + ATT_HD * head + d)
    k0 = q0 + ATT_W
    for g in range(ATT_KV):
        for l in range(LANES):
            quarter, i = divmod(l, half)
            d = i + half * (quarter // 2)
            idx.append(k0 + ATT_HD * g + d)
    v0 = k0 + ATT_KV * ATT_HD
    for g in range(ATT_KV):
        for l in range(LANES):
            idx.append(v0 + ATT_HD * g + (l % ATT_HD))
    assert len(idx) == NW
    return np.asarray(idx, dtype=np.int32)


def _up_proj_perm():
    idx = []
    for j in range(N_FF_TILES):
        idx += list(range(j * FF_TILE, (j + 1) * FF_TILE))
        idx += list(range(D_FF + j * FF_TILE, D_FF + (j + 1) * FF_TILE))
    return np.asarray(idx, dtype=np.int32)


def _sigmoid(v):
    return 1.0 / (1.0 + jnp.exp(-v))


def _dot(a, b):
    return jnp.dot(a, b, preferred_element_type=F32)


def _dot_nt(a, b):
    return lax.dot_general(a, b, (((1,), (1,)), ((), ())), preferred_element_type=F32)


def _dot_tn(a, b):
    return lax.dot_general(a, b, (((0,), (0,)), ((), ())), preferred_element_type=F32)


def _rms(v):
    return v * lax.rsqrt(jnp.mean(v * v, axis=-1, keepdims=True) + RMS_EPS)


def _adaln_kernel(c_ref, w_ref, b_ref, o_ref):
    c = c_ref[...]
    ca = (c * _sigmoid(c)).astype(BF16)
    o_ref[...] = _dot(ca, w_ref[...].astype(BF16)) + b_ref[...]


def _adaln(c, w, b):
    bsz, d = c.shape
    n = w.shape[1]
    tn = 1024
    return pl.pallas_call(
        _adaln_kernel,
        grid=(n // tn,),
        in_specs=[
            pl.BlockSpec((bsz, d), lambda j: (0, 0)),
            pl.BlockSpec((d, tn), lambda j: (0, j)),
            pl.BlockSpec((1, tn), lambda j: (0, j)),
        ],
        out_specs=pl.BlockSpec((bsz, tn), lambda j: (0, j)),
        out_shape=jax.ShapeDtypeStruct((bsz, n), F32),
        compiler_params=pltpu.CompilerParams(dimension_semantics=("arbitrary",)),
        name="adaln",
    )(c, w, b.reshape(1, n))


def _mixer_kernel(x_ref, mod_ref, gpre_ref, win_ref, cw_ref, cb_ref, gb_ref, mlg_ref, sink_ref,
                  atg_ref, cos_ref, sin_ref, wout_ref, gpost_ref, o_ref,
                  h_s, xp_s, q_s, k_s, v_s, og_s, ip_s, bc_s, ipt_s, bt_s, qa_s, kt_s, vt_s,
                  mix_s, st_s, m_s):
    s_idx = pl.program_id(1)
    ts = x_ref.shape[0]
    nblk = ts // BLK

    @pl.when(s_idx == 0)
    def _reset_state():
        xp_s[0:SUBLANES, :] = jnp.zeros((SUBLANES, 2 * ML_W), F32)
        kt_s[:, 0:BLK, :] = jnp.zeros((ATT_KV, BLK, LANES), BF16)
        vt_s[:, 0:BLK, :] = jnp.zeros((ATT_KV, BLK, LANES), BF16)
        st_s[...] = jnp.zeros(st_s.shape, F32)
        m_s[...] = jnp.zeros(m_s.shape, F32)

    x = x_ref[...]
    h = (_rms(x) * gpre_ref[...]) * (1.0 + mod_ref[1:2, :]) + mod_ref[0:1, :]
    h_s[...] = h.astype(BF16)

    kscale = 1.0 / math.sqrt(ML_HD)
    cchunk = 256
    for cj in range(2 * ML_W // cchunk):
        cols = slice(cj * cchunk, (cj + 1) * cchunk)
        u = _dot(h_s[...], win_ref[:, C_QK + cj * cchunk:C_QK + (cj + 1) * cchunk])
        xp_s[SUBLANES:SUBLANES + ts, cols] = u
        w = cw_ref[:, cols]
        y = (u * w[3:4] + xp_s[SUBLANES - 1:SUBLANES - 1 + ts, cols] * w[2:3]
             + xp_s[SUBLANES - 2:SUBLANES - 2 + ts, cols] * w[1:2]
             + xp_s[SUBLANES - 3:SUBLANES - 3 + ts, cols] * w[0:1] + cb_ref[:, cols])
        a = y * _sigmoid(y)
        if cj * cchunk < ML_W:
            q_s[:, cols] = a.astype(BF16)
        else:
            k_s[:, cj * cchunk - ML_W:(cj + 1) * cchunk - ML_W] = (a * kscale).astype(BF16)
        xp_s[0:SUBLANES, cols] = u[ts - SUBLANES:ts, :]

    v_s[...] = _dot(h_s[...], win_ref[:, C_V:C_V + ML_W]).astype(BF16)
    og_s[...] = _dot(h_s[...], win_ref[:, C_O:C_O + ML_W])

    gts = _dot(h_s[...], win_ref[:, C_G:C_G + LANES]) + gb_ref[...]
    lsg = jnp.minimum(gts, 0.0) - jnp.log(1.0 + jnp.exp(-jnp.abs(gts)))
    row_i = lax.broadcasted_iota(jnp.int32, (BLK, BLK), 0)
    col_i = lax.broadcasted_iota(jnp.int32, (BLK, BLK), 1)
    tril = row_i >= col_i
    tril_b = jnp.where(tril, 1.0, 0.0).astype(BF16)
    for r in range(nblk):
        ipr = gts[r * BLK:(r + 1) * BLK, :]
        lsr = lsg[r * BLK:(r + 1) * BLK, :]
        hi = lsr.astype(BF16)
        r1 = lsr - hi.astype(F32)
        mid = r1.astype(BF16)
        lo = (r1 - mid.astype(F32)).astype(BF16)
        bc = _dot(tril_b, hi) + _dot(tril_b, mid) + _dot(tril_b, lo)
        ip_s[r] = ipr
        bc_s[r] = bc
        ipt_s[r] = ipr.T[0:SUBLANES, :]
        bt_s[r] = bc.T[0:SUBLANES, :]

    cos = cos_ref[...]
    sin = sin_ref[...]
    qat = _dot(h_s[...], win_ref[:, C_QA:C_QA + ATT_W])
    qscale = 1.0 / math.sqrt(ATT_HD)
    for p in range(ATT_H // 2):
        qp = qat[:, p * LANES:(p + 1) * LANES]
        qr = (qp * cos + pltpu.roll(qp, LANES // 2, 1) * sin) * qscale
        qa_s[:, p * LANES:(p + 1) * LANES] = qr.astype(BF16)
    kat = _dot(h_s[...], win_ref[:, C_KA:C_KA + ATT_KV * LANES])
    vat = _dot(h_s[...], win_ref[:, C_VA:C_VA + ATT_KV * LANES])
    for g in range(ATT_KV):
        kp = kat[:, g * LANES:(g + 1) * LANES]
        kr = kp * cos + pltpu.roll(kp, LANES // 2, 1) * sin
        kt_s[g, BLK:BLK + ts, :] = kr.astype(BF16)
        vt_s[g, BLK:BLK + ts, :] = vat[:, g * LANES:(g + 1) * LANES].astype(BF16)

    lane = lax.broadcasted_iota(jnp.int32, (BLK, LANES), 1)
    head_a = (lane % (LANES // 2)) < (LANES // 4)
    lane_lo = lane < (LANES // 2)
    e0 = jnp.where(lane == 0, 1.0, 0.0).astype(BF16)
    rows4 = lax.broadcasted_iota(jnp.int32, (4 * BLK, 2 * BLK), 0) % BLK
    cols4 = lax.broadcasted_iota(jnp.int32, (4 * BLK, 2 * BLK), 1)
    own = cols4 >= BLK
    band_own = own & ((cols4 - BLK) <= rows4)
    band_prev = (~own) & (cols4 > rows4)

    def block(r, carry):
        off = pl.multiple_of(r * BLK, BLK)
        rows = pl.ds(off, BLK)
        ip = ip_s[r]
        bc = bc_s[r]
        ipt = ipt_s[r]
        bt = bt_s[r]
        m_all = m_s[...]
        for j in range(ML_H):
            hc = slice(j * ML_HD, (j + 1) * ML_HD)
            q = q_s[rows, hc]
            k = k_s[rows, hc]
            v = v_s[rows, hc]
            i_col = ip[:, j:j + 1]
            b_col = bc[:, ML_H + j:ML_H + j + 1]
            i_row = ipt[j:j + 1, :]
            b_row = bt[ML_H + j:ML_H + j + 1, :]
            m_prev = m_all[j:j + 1, 0:1]
            b_last = b_col[BLK - 1:BLK, :]
            dm = jnp.where(tril, b_col + (i_row - b_row), NEG)
            m_inter = b_col + m_prev
            m_t = jnp.maximum(m_inter, jnp.max(dm, axis=-1, keepdims=True))
            w_inter = jnp.exp(m_inter - m_t)
            sqk = (_dot_nt(q, k) * jnp.exp(dm - m_t)).astype(BF16)
            v_aug = jnp.concatenate([v, e0], axis=1)
            st = st_s[j]
            num = _dot(sqk, v_aug) + w_inter * _dot(q, st.astype(BF16))
            den = num[:, ML_HD:ML_HD + 1]
            hv = num[:, 0:ML_HD] / jnp.maximum(jnp.abs(den), jnp.exp(-m_t))
            hn = _rms(hv) * mlg_ref[:, hc]
            mix_s[rows, hc] = (hn * _sigmoid(og_s[rows, hc])).astype(BF16)
            g_col = b_last - b_col + i_col
            m_new = jnp.maximum(b_last + m_prev, jnp.max(g_col, axis=0, keepdims=True))
            decay = jnp.exp(b_last + m_prev - m_new)
            kw = (k.astype(F32) * jnp.exp(g_col - m_new)).astype(BF16)
            st_s[j] = decay * st + _dot_tn(kw, v_aug)
            m_s[j:j + 1, :] = jnp.broadcast_to(m_new, (1, LANES))

        not_first = (s_idx * nblk + r) > 0
        valid = band_own | (band_prev & not_first)
        pair_out = []
        for g in range(ATT_KV):
            kb = kt_s[g, pl.ds(off, 2 * BLK), :]
            vb = vt_s[g, pl.ds(off, 2 * BLK), :]
            qs = []
            sk = []
            for pp in range(2):
                p = 2 * g + pp
                qp = qa_s[rows, p * LANES:(p + 1) * LANES]
                qs.append(jnp.where(head_a, qp, jnp.zeros_like(qp)))
                qs.append(jnp.where(head_a, jnp.zeros_like(qp), qp))
                for hh in range(2):
                    sk.append(jnp.broadcast_to(sink_ref[2 * p + hh:2 * p + hh + 1, 0:1], (BLK, 1)))
            q4 = jnp.concatenate(qs, axis=0)
            sink4 = jnp.concatenate(sk, axis=0)
            sc = jnp.where(valid, _dot_nt(q4, kb), NEG)
            m = jnp.maximum(jnp.max(sc, axis=-1, keepdims=True), sink4)
            pe = jnp.exp(sc - m)
            denom = jnp.sum(pe, axis=-1, keepdims=True) + jnp.exp(sink4 - m)
            o4 = _dot(pe.astype(BF16), vb) / denom
            for pp in range(2):
                pair_out.append(jnp.where(lane_lo, o4[(2 * pp) * BLK:(2 * pp + 1) * BLK, :],
                                          o4[(2 * pp + 1) * BLK:(2 * pp + 2) * BLK, :]))
        ssq = sum(jnp.sum(po * po, axis=-1, keepdims=True) for po in pair_out)
        rs = lax.rsqrt(ssq * (1.0 / ATT_W) + RMS_EPS)
        for p in range(ATT_H // 2):
            mix_s[rows, ML_W + p * LANES:ML_W + (p + 1) * LANES] = (
                pair_out[p] * rs * atg_ref[:, p * LANES:(p + 1) * LANES]).astype(BF16)
        return carry

    lax.fori_loop(0, nblk, block, 0)

    for g in range(ATT_KV):
        kt_s[g, 0:BLK, :] = kt_s[g, ts:ts + BLK, :]
        vt_s[g, 0:BLK, :] = vt_s[g, ts:ts + BLK, :]

    y = _dot(mix_s[...], wout_ref[...])
    o_ref[...] = x_ref[...] + mod_ref[2:3, :] * (_rms(y) * gpost_ref[...])


def _const_spec(shape):
    nd = len(shape)
    return pl.BlockSpec(shape, lambda b, s: (0,) * nd, pipeline_mode=pl.Buffered(1))


def _mixer(x, mod, gpre, win, cw, cb, gb, mlg, sinks, atg, cos, sin, wout, gpost, ts):
    bsz, seq, d = x.shape
    nblk = ts // BLK
    in_specs = [
        pl.BlockSpec((None, ts, d), lambda b, s: (b, s, 0)),
        pl.BlockSpec((None, N_MOD, d), lambda b, s: (b, 0, 0)),
        _const_spec((1, d)),
        _const_spec((d, NW)),
        _const_spec((ML_CONV, 2 * ML_W)),
        _const_spec((1, 2 * ML_W)),
        _const_spec((1, LANES)),
        _const_spec((1, ML_W)),
        _const_spec((ATT_H, LANES)),
        _const_spec((1, ATT_W)),
        pl.BlockSpec((ts, LANES), lambda b, s: (s, 0)),
        pl.BlockSpec((ts, LANES), lambda b, s: (s, 0)),
        _const_spec((MIX_W, d)),
        _const_spec((1, d)),
    ]
    scratch = [
        pltpu.VMEM((ts, d), BF16),
        pltpu.VMEM((SUBLANES + ts, 2 * ML_W), F32),
        pltpu.VMEM((ts, ML_W), BF16),
        pltpu.VMEM((ts, ML_W), BF16),
        pltpu.VMEM((ts, ML_W), BF16),
        pltpu.VMEM((ts, ML_W), F32),
        pltpu.VMEM((nblk, BLK, LANES), F32),
        pltpu.VMEM((nblk, BLK, LANES), F32),
        pltpu.VMEM((nblk, SUBLANES, BLK), F32),
        pltpu.VMEM((nblk, SUBLANES, BLK), F32),
        pltpu.VMEM((ts, ATT_W), BF16),
        pltpu.VMEM((ATT_KV, BLK + ts, LANES), BF16),
        pltpu.VMEM((ATT_KV, BLK + ts, LANES), BF16),
        pltpu.VMEM((ts, MIX_W), BF16),
        pltpu.VMEM((ML_H, ML_HD, 2 * ML_HD), F32),
        pltpu.VMEM((SUBLANES, LANES), F32),
    ]
    return pl.pallas_call(
        _mixer_kernel,
        grid=(bsz, seq // ts),
        in_specs=in_specs,
        out_specs=pl.BlockSpec((None, ts, d), lambda b, s: (b, s, 0)),
        out_shape=jax.ShapeDtypeStruct((bsz, seq, d), F32),
        scratch_shapes=scratch,
        compiler_params=pltpu.CompilerParams(
            dimension_semantics=("arbitrary", "arbitrary"), vmem_limit_bytes=VMEM_LIMIT),
        name="mixer",
    )(x, mod, gpre, win, cw, cb, gb, mlg, sinks, atg, cos, sin, wout, gpost)


def _ffn_kernel(x_ref, mod_ref, gpre_ref, wup_ref, cw_ref, cb_ref, wdn_ref, gpost_ref, o_ref,
                h_s, ub_s, carry_s, act_s):
    s_idx = pl.program_id(1)
    ts = x_ref.shape[0]
    tw = 2 * FF_TILE

    @pl.when(s_idx == 0)
    def _reset_state():
        carry_s[...] = jnp.zeros(carry_s.shape, F32)

    x = x_ref[...]
    h = (_rms(x) * gpre_ref[...]) * (1.0 + mod_ref[4:5, :]) + mod_ref[3:4, :]
    h_s[...] = h.astype(BF16)

    c1 = math.sqrt(2.0 / math.pi)
    for j in range(N_FF_TILES):
        cols = slice(j * tw, (j + 1) * tw)
        sl = j % 2
        u = _dot(h_s[...], wup_ref[:, cols])
        ub_s[sl, 0:SUBLANES, :] = carry_s[j]
        ub_s[sl, SUBLANES:SUBLANES + ts, :] = u
        carry_s[j] = u[ts - SUBLANES:ts, :]
        w = cw_ref[:, cols]
        y = (u * w[2:3] + ub_s[sl, SUBLANES - 1:SUBLANES - 1 + ts, :] * w[1:2]
             + ub_s[sl, SUBLANES - 2:SUBLANES - 2 + ts, :] * w[0:1] + cb_ref[:, cols])
        g = y[:, 0:FF_TILE]
        val = y[:, FF_TILE:tw]
        cdf = 0.5 * (1.0 + jnp.tanh(c1 * (g + 0.044715 * (g * g * g))))
        act_s[:, j * FF_TILE:(j + 1) * FF_TILE] = (g * cdf * val).astype(BF16)

    y2 = _dot(act_s[...], wdn_ref[...])
    o_ref[...] = x_ref[...] + mod_ref[5:6, :] * (_rms(y2) * gpost_ref[...])


def _ffn(x, mod, gpre, wup, cw, cb, wdn, gpost, ts):
    bsz, seq, d = x.shape
    tw = 2 * FF_TILE
    in_specs = [
        pl.BlockSpec((None, ts, d), lambda b, s: (b, s, 0)),
        pl.BlockSpec((None, N_MOD, d), lambda b, s: (b, 0, 0)),
        _const_spec((1, d)),
        _const_spec((d, 2 * D_FF)),
        _const_spec((FFN_CONV, 2 * D_FF)),
        _const_spec((1, 2 * D_FF)),
        _const_spec((D_FF, d)),
        _const_spec((1, d)),
    ]
    scratch = [
        pltpu.VMEM((ts, d), BF16),
        pltpu.VMEM((2, SUBLANES + ts, tw), F32),
        pltpu.VMEM((N_FF_TILES, SUBLANES, tw), F32),
        pltpu.VMEM((ts, D_FF), BF16),
    ]
    return pl.pallas_call(
        _ffn_kernel,
        grid=(bsz, seq // ts),
        in_specs=in_specs,
        out_specs=pl.BlockSpec((None, ts, d), lambda b, s: (b, s, 0)),
        out_shape=jax.ShapeDtypeStruct((bsz, seq, d), F32),
        scratch_shapes=scratch,
        compiler_params=pltpu.CompilerParams(
            dimension_semantics=("arbitrary", "arbitrary"), vmem_limit_bytes=VMEM_LIMIT),
        name="conv_ffn",
    )(x, mod, gpre, wup, cw, cb, wdn, gpost)


def _rope_tables(seq):
    half = ATT_HD // 2
    inv = ROPE_THETA ** (-jnp.arange(half, dtype=F32) / half)
    ang = jnp.arange(seq).astype(F32)[:, None] * inv[None, :]
    cos = jnp.cos(ang)
    sin = jnp.sin(ang)
    return jnp.tile(cos, (1, 4)), jnp.concatenate([-sin, -sin, sin, sin], axis=1)


def kernel(x, c, w_ada, b_ada, pre_mix_g, w_in, ml_conv_w, ml_conv_b, ml_i_b, ml_f_b, ml_norm_g,
           attn_sinks, attn_norm_g, w_out, post_mix_g, pre_ffn_g, w_up, ffn_conv_w, ffn_conv_b,
           w_down, post_ffn_g):
    bsz, seq, d = x.shape
    depth = w_ada.shape[0]
    ts = min(512, seq)
    in_perm = _in_proj_perm()
    up_perm = _up_proj_perm()
    cos, sin = _rope_tables(seq)
    for l in range(depth):
        mod = _adaln(c, w_ada[l], b_ada[l]).reshape(bsz, N_MOD, d)
        win = jnp.concatenate([w_in[l], jnp.zeros((d, 1), F32)], axis=1)[:, in_perm].astype(BF16)
        gb = jnp.concatenate([ml_i_b[l], ml_f_b[l], jnp.zeros((LANES - 2 * ML_H,), F32)]).reshape(1, LANES)
        sinks = jnp.broadcast_to(attn_sinks[l][:, None], (ATT_H, LANES))
        x = _mixer(x, mod, pre_mix_g[l].reshape(1, d), win, ml_conv_w[l], ml_conv_b[l].reshape(1, -1),
                   gb, ml_norm_g[l].reshape(1, ML_W), sinks, attn_norm_g[l].reshape(1, ATT_W),
                   cos, sin, w_out[l].astype(BF16), post_mix_g[l].reshape(1, d), ts)
        x = _ffn(x, mod, pre_ffn_g[l].reshape(1, d), w_up[l][:, up_perm].astype(BF16),
                 ffn_conv_w[l][:, up_perm], ffn_conv_b[l][up_perm].reshape(1, -1),
                 w_down[l].astype(BF16), post_ffn_g[l].reshape(1, d), ts)
    return x
```

```python
import functools
import math

import numpy as np
import jax
import jax.numpy as jnp
from jax import lax
from jax.experimental import pallas as pl
from jax.experimental.pallas import tpu as pltpu

F32 = jnp.float32
BF16 = jnp.bfloat16

D_MODEL = 1024
ML_H = 4
ML_HD = 128
ML_W = ML_H * ML_HD
ML_CONV = 4
ATT_H = 8
ATT_KV = 2
ATT_HD = 64
ATT_W = ATT_H * ATT_HD
WINDOW = 128
ROPE_THETA = 10000.0
MIX_W = ML_W + ATT_W
IN_COLS = 4 * ML_W + 2 * ML_H + ATT_W + 2 * ATT_KV * ATT_HD
D_FF = 2816
FFN_CONV = 3
N_MOD = 6
RMS_EPS = 1e-6

LANES = 128
SUBLANES = 8
BLK = 128
NEG = -1e30

C_QK = 0
C_V = 2 * ML_W
C_O = 3 * ML_W
C_G = 4 * ML_W
C_QA = C_G + LANES
C_KA = C_QA + ATT_W
C_VA = C_KA + ATT_KV * LANES
NW = C_VA + ATT_KV * LANES

FF_TILE = 256
N_FF_TILES = D_FF // FF_TILE

VMEM_LIMIT = 56 * 1024 * 1024


def _in_proj_perm():
    zero = IN_COLS
    idx = list(range(0, 4 * ML_W))
    g0 = 4 * ML_W
    idx += [g0 + j for j in range(2 * ML_H)] + [zero] * (LANES - 2 * ML_H)
    q0 = g0 + 2 * ML_H
    half = ATT_HD // 2
    for p in range(ATT_H // 2):
        for l in range(LANES):
            quarter, i = divmod(l, half)
            head = 2 * p + (quarter % 2)
            d = i + half * (quarter // 2)
            idx.append(q0 + ATT_HD * head + d)
    k0 = q0 + ATT_W
    for g in range(ATT_KV):
        for l in range(LANES):
            quarter, i = divmod(l, half)
            d = i + half * (quarter // 2)
            idx.append(k0 + ATT_HD * g + d)
    v0 = k0 + ATT_KV * ATT_HD
    for g in range(ATT_KV):
        for l in range(LANES):
            idx.append(v0 + ATT_HD * g + (l % ATT_HD))
    assert len(idx) == NW
    return np.asarray(idx, dtype=np.int32)


def _up_proj_perm():
    idx = []
    for j in range(N_FF_TILES):
        idx += list(range(j * FF_TILE, (j + 1) * FF_TILE))
        idx += list(range(D_FF + j * FF_TILE, D_FF + (j + 1) * FF_TILE))
    return np.asarray(idx, dtype=np.int32)


def _sigmoid(v):
    return 1.0 / (1.0 + jnp.exp(-v))


def _dot(a, b):
    return jnp.dot(a, b, preferred_element_type=F32)


def _dot_nt(a, b):
    return lax.dot_general(a, b, (((1,), (1,)), ((), ())), preferred_element_type=F32)


def _dot_tn(a, b):
    return lax.dot_general(a, b, (((0,), (0,)), ((), ())), preferred_element_type=F32)


def _rms(v):
    return v * lax.rsqrt(jnp.mean(v * v, axis=-1, keepdims=True) + RMS_EPS)


def _adaln_kernel(c_ref, w_ref, b_ref, o_ref):
    c = c_ref[...]
    ca = (c * _sigmoid(c)).astype(BF16)
    o_ref[...] = _dot(ca, w_ref[...].astype(BF16)) + b_ref[...]


def _adaln(c, w, b):
    bsz, d = c.shape
    n = w.shape[1]
    tn = 1024
    return pl.pallas_call(
        _adaln_kernel,
        grid=(n // tn,),
        in_specs=[
            pl.BlockSpec((bsz, d), lambda j: (0, 0)),
            pl.BlockSpec((d, tn), lambda j: (0, j)),
            pl.BlockSpec((1, tn), lambda j: (0, j)),
        ],
        out_specs=pl.BlockSpec((bsz, tn), lambda j: (0, j)),
        out_shape=jax.ShapeDtypeStruct((bsz, n), F32),
        compiler_params=pltpu.CompilerParams(dimension_semantics=("arbitrary",)),
        name="adaln",
    )(c, w, b.reshape(1, n))


def _mixer_kernel(x_ref, mod_ref, gpre_ref, win_ref, cw_ref, cb_ref, gb_ref, mlg_ref, sink_ref,
                  atg_ref, cos_ref, sin_ref, wout_ref, gpost_ref, o_ref,
                  h_s, xp_s, q_s, kt_ml_s, v_s, og_s, bc_s, ipt_s, bt_s, qa_s, kt_s, vt_s,
                  mix_s, st_s, m_s):
    s_idx = pl.program_id(1)
    ts = x_ref.shape[0]
    nblk = ts // BLK

    @pl.when(s_idx == 0)
    def _reset_state():
        xp_s[0:SUBLANES, :] = jnp.zeros((SUBLANES, 2 * ML_W), F32)
        kt_s[:, 0:BLK, :] = jnp.zeros((ATT_KV, BLK, LANES), BF16)
        vt_s[:, :, 0:LANES] = jnp.zeros((ATT_KV, BLK + ts, LANES), BF16)
        vt_s[:, :, LANES:2 * LANES] = jnp.ones((ATT_KV, BLK + ts, LANES), BF16)
        st_s[...] = jnp.zeros(st_s.shape, F32)
        m_s[...] = jnp.zeros(m_s.shape, F32)

    x = x_ref[...]
    h = (_rms(x) * gpre_ref[...]) * (1.0 + mod_ref[1:2, :]) + mod_ref[0:1, :]
    h_s[...] = h.astype(BF16)

    kscale = 1.0 / math.sqrt(ML_HD)
    cchunk = 256
    for cj in range(2 * ML_W // cchunk):
        cols = slice(cj * cchunk, (cj + 1) * cchunk)
        u = _dot(h_s[...], win_ref[:, C_QK + cj * cchunk:C_QK + (cj + 1) * cchunk])
        xp_s[SUBLANES:SUBLANES + ts, cols] = u
        w = cw_ref[:, cols]
        y = (u * w[3:4] + xp_s[SUBLANES - 1:SUBLANES - 1 + ts, cols] * w[2:3]
             + xp_s[SUBLANES - 2:SUBLANES - 2 + ts, cols] * w[1:2]
             + xp_s[SUBLANES - 3:SUBLANES - 3 + ts, cols] * w[0:1] + cb_ref[:, cols])
        a = y * _sigmoid(y)
        if cj * cchunk < ML_W:
            q_s[:, cols] = a.astype(BF16)
        else:
            ak = a * kscale
            for jj in range(cchunk // ML_HD):
                j = (cj * cchunk - ML_W) // ML_HD + jj
                for r in range(nblk):
                    kt_ml_s[r, j] = ak[r * BLK:(r + 1) * BLK, jj * ML_HD:(jj + 1) * ML_HD].T.astype(BF16)
        xp_s[0:SUBLANES, cols] = u[ts - SUBLANES:ts, :]

    v_s[...] = _dot(h_s[...], win_ref[:, C_V:C_V + ML_W]).astype(BF16)
    og_s[...] = _dot(h_s[...], win_ref[:, C_O:C_O + ML_W])

    gts = _dot(h_s[...], win_ref[:, C_G:C_G + LANES]) + gb_ref[...]
    lsg = jnp.minimum(gts, 0.0) - jnp.log(1.0 + jnp.exp(-jnp.abs(gts)))
    row_i = lax.broadcasted_iota(jnp.int32, (BLK, BLK), 0)
    col_i = lax.broadcasted_iota(jnp.int32, (BLK, BLK), 1)
    tril = row_i >= col_i
    tril_b = jnp.where(tril, 1.0, 0.0).astype(BF16)
    for r in range(nblk):
        ipr = gts[r * BLK:(r + 1) * BLK, :]
        lsr = lsg[r * BLK:(r + 1) * BLK, :]
        hi = lsr.astype(BF16)
        r1 = lsr - hi.astype(F32)
        mid = r1.astype(BF16)
        lo = (r1 - mid.astype(F32)).astype(BF16)
        bc = _dot(tril_b, hi) + _dot(tril_b, mid) + _dot(tril_b, lo)
        bc_s[r] = bc
        ipt_s[r] = ipr.T[0:SUBLANES, :]
        bt_s[r] = bc.T[0:SUBLANES, :]

    cos = cos_ref[...]
    sin = sin_ref[...]
    qat = _dot(h_s[...], win_ref[:, C_QA:C_QA + ATT_W])
    qscale = 1.0 / math.sqrt(ATT_HD)
    for p in range(ATT_H // 2):
        qp = qat[:, p * LANES:(p + 1) * LANES]
        qr = (qp * cos + pltpu.roll(qp, LANES // 2, 1) * sin) * qscale
        qa_s[:, p * LANES:(p + 1) * LANES] = qr.astype(BF16)
    kat = _dot(h_s[...], win_ref[:, C_KA:C_KA + ATT_KV * LANES])
    vat = _dot(h_s[...], win_ref[:, C_VA:C_VA + ATT_KV * LANES])
    for g in range(ATT_KV):
        kp = kat[:, g * LANES:(g + 1) * LANES]
        kr = kp * cos + pltpu.roll(kp, LANES // 2, 1) * sin
        kt_s[g, BLK:BLK + ts, :] = kr.astype(BF16)
        vt_s[g, BLK:BLK + ts, 0:LANES] = vat[:, g * LANES:(g + 1) * LANES].astype(BF16)

    lane = lax.broadcasted_iota(jnp.int32, (BLK, LANES), 1)
    head_a = (lane % (LANES // 2)) < (LANES // 4)
    lane_lo = lane < (LANES // 2)
    ones_b = jnp.ones((BLK, LANES), BF16)
    rows4 = lax.broadcasted_iota(jnp.int32, (4 * BLK, 2 * BLK), 0) % BLK
    cols4 = lax.broadcasted_iota(jnp.int32, (4 * BLK, 2 * BLK), 1)
    own = cols4 >= BLK
    band_own = own & ((cols4 - BLK) <= rows4)
    band_prev = (~own) & (cols4 > rows4)

    def block(r, carry):
        off = pl.multiple_of(r * BLK, BLK)
        rows = pl.ds(off, BLK)
        bc = bc_s[r]
        ipt = ipt_s[r]
        bt = bt_s[r]
        m_all = m_s[...]
        for j in range(ML_H):
            hc = slice(j * ML_HD, (j + 1) * ML_HD)
            q = q_s[rows, hc]
            kt = kt_ml_s[r, j]
            v_aug = jnp.concatenate([v_s[rows, hc], ones_b], axis=1)
            b_col = bc[:, ML_H + j:ML_H + j + 1]
            i_row = ipt[j:j + 1, :]
            b_row = bt[ML_H + j:ML_H + j + 1, :]
            m_prev = m_all[j:j + 1, :]
            m_prev1 = m_prev[:, 0:1]
            b_last = b_col[BLK - 1:BLK, :]
            b_rep = jnp.broadcast_to(b_col, (BLK, LANES))
            dm = jnp.where(tril, b_rep + (i_row - b_row), NEG)
            m_t_col = jnp.maximum(b_col + m_prev1, jnp.max(dm, axis=-1, keepdims=True))
            m_t = jnp.broadcast_to(m_t_col, (BLK, LANES))
            w_inter = jnp.exp(b_rep + m_prev - m_t)
            sqk = (_dot(q, kt) * jnp.exp(dm - m_t)).astype(BF16)
            st = st_s[j]
            num = _dot(sqk, v_aug) + jnp.concatenate([w_inter, w_inter], axis=1) * _dot(q, st.astype(BF16))
            den = num[:, ML_HD:2 * ML_HD]
            hv = num[:, 0:ML_HD] / jnp.maximum(jnp.abs(den), jnp.exp(-m_t))
            ms = _dot((hv * hv).astype(BF16), ones_b) * (1.0 / ML_HD)
            hn = hv * lax.rsqrt(ms + RMS_EPS) * mlg_ref[:, hc]
            mix_s[rows, hc] = (hn * _sigmoid(og_s[rows, hc])).astype(BF16)
            g_row = b_last - b_row + i_row
            m_new = jnp.maximum(b_last + m_prev1, jnp.max(g_row, axis=-1, keepdims=True))
            decay = jnp.exp(b_last + m_prev1 - m_new)
            kw = (kt.astype(F32) * jnp.exp(g_row - m_new)).astype(BF16)
            st_s[j] = decay * st + _dot(kw, v_aug)
            m_s[j:j + 1, :] = jnp.broadcast_to(m_new, (1, LANES))

        not_first = (s_idx * nblk + r) > 0
        valid = band_own | (band_prev & not_first)
        pair_out = []
        for g in range(ATT_KV):
            kb = kt_s[g, pl.ds(off, 2 * BLK), :]
            vb = vt_s[g, pl.ds(off, 2 * BLK), :]
            qs = []
            sk = []
            for pp in range(2):
                p = 2 * g + pp
                qp = qa_s[rows, p * LANES:(p + 1) * LANES]
                qs.append(jnp.where(head_a, qp, jnp.zeros_like(qp)))
                qs.append(jnp.where(head_a, jnp.zeros_like(qp), qp))
                for hh in range(2):
                    sk.append(jnp.broadcast_to(sink_ref[2 * p + hh:2 * p + hh + 1, :], (BLK, LANES)))
            q4 = jnp.concatenate(qs, axis=0)
            sink4 = jnp.concatenate(sk, axis=0)
            sc = jnp.where(valid, _dot_nt(q4, kb), NEG)
            m_col = jnp.max(sc, axis=-1, keepdims=True)
            m = jnp.maximum(jnp.broadcast_to(m_col, (4 * BLK, LANES)), sink4)
            pe = jnp.exp(sc - jnp.concatenate([m, m], axis=1))
            o4a = _dot(pe.astype(BF16), vb)
            o4 = o4a[:, 0:LANES] / (o4a[:, LANES:2 * LANES] + jnp.exp(sink4 - m))
            for pp in range(2):
                pair_out.append(jnp.where(lane_lo, o4[(2 * pp) * BLK:(2 * pp + 1) * BLK, :],
                                          o4[(2 * pp + 1) * BLK:(2 * pp + 2) * BLK, :]))
        sq = sum(po * po for po in pair_out)
        rs = lax.rsqrt(_dot(sq.astype(BF16), ones_b) * (1.0 / ATT_W) + RMS_EPS)
        for p in range(ATT_H // 2):
            mix_s[rows, ML_W + p * LANES:ML_W + (p + 1) * LANES] = (
                pair_out[p] * rs * atg_ref[:, p * LANES:(p + 1) * LANES]).astype(BF16)
        return carry

    lax.fori_loop(0, nblk, block, 0)

    for g in range(ATT_KV):
        kt_s[g, 0:BLK, :] = kt_s[g, ts:ts + BLK, :]
        vt_s[g, 0:BLK, 0:LANES] = vt_s[g, ts:ts + BLK, 0:LANES]

    y = _dot(mix_s[...], wout_ref[...])
    o_ref[...] = x_ref[...] + mod_ref[2:3, :] * (_rms(y) * gpost_ref[...])


def _const_spec(shape):
    nd = len(shape)
    return pl.BlockSpec(shape, lambda b, s: (0,) * nd, pipeline_mode=pl.Buffered(1))


def _mixer(x, mod, gpre, win, cw, cb, gb, mlg, sinks, atg, cos, sin, wout, gpost, ts):
    bsz, seq, d = x.shape
    nblk = ts // BLK
    in_specs = [
        pl.BlockSpec((None, ts, d), lambda b, s: (b, s, 0)),
        pl.BlockSpec((None, N_MOD, d), lambda b, s: (b, 0, 0)),
        _const_spec((1, d)),
        _const_spec((d, NW)),
        _const_spec((ML_CONV, 2 * ML_W)),
        _const_spec((1, 2 * ML_W)),
        _const_spec((1, LANES)),
        _const_spec((1, ML_W)),
        _const_spec((ATT_H, LANES)),
        _const_spec((1, ATT_W)),
        pl.BlockSpec((ts, LANES), lambda b, s: (s, 0)),
        pl.BlockSpec((ts, LANES), lambda b, s: (s, 0)),
        _const_spec((MIX_W, d)),
        _const_spec((1, d)),
    ]
    scratch = [
        pltpu.VMEM((ts, d), BF16),
        pltpu.VMEM((SUBLANES + ts, 2 * ML_W), F32),
        pltpu.VMEM((ts, ML_W), BF16),
        pltpu.VMEM((nblk, ML_H, ML_HD, BLK), BF16),
        pltpu.VMEM((ts, ML_W), BF16),
        pltpu.VMEM((ts, ML_W), F32),
        pltpu.VMEM((nblk, BLK, LANES), F32),
        pltpu.VMEM((nblk, SUBLANES, BLK), F32),
        pltpu.VMEM((nblk, SUBLANES, BLK), F32),
        pltpu.VMEM((ts, ATT_W), BF16),
        pltpu.VMEM((ATT_KV, BLK + ts, LANES), BF16),
        pltpu.VMEM((ATT_KV, BLK + ts, 2 * LANES), BF16),
        pltpu.VMEM((ts, MIX_W), BF16),
        pltpu.VMEM((ML_H, ML_HD, 2 * ML_HD), F32),
        pltpu.VMEM((SUBLANES, LANES), F32),
    ]
    return pl.pallas_call(
        _mixer_kernel,
        grid=(bsz, seq // ts),
        in_specs=in_specs,
        out_specs=pl.BlockSpec((None, ts, d), lambda b, s: (b, s, 0)),
        out_shape=jax.ShapeDtypeStruct((bsz, seq, d), F32),
        scratch_shapes=scratch,
        compiler_params=pltpu.CompilerParams(
            dimension_semantics=("arbitrary", "arbitrary"), vmem_limit_bytes=VMEM_LIMIT),
        name="mixer",
    )(x, mod, gpre, win, cw, cb, gb, mlg, sinks, atg, cos, sin, wout, gpost)


def _ffn_kernel(x_ref, mod_ref, gpre_ref, wup_ref, cw_ref, cb_ref, wdn_ref, gpost_ref, o_ref,
                h_s, ub_s, carry_s, act_s):
    s_idx = pl.program_id(1)
    ts = x_ref.shape[0]
    tw = 2 * FF_TILE

    @pl.when(s_idx == 0)
    def _reset_state():
        carry_s[...] = jnp.zeros(carry_s.shape, F32)

    x = x_ref[...]
    h = (_rms(x) * gpre_ref[...]) * (1.0 + mod_ref[4:5, :]) + mod_ref[3:4, :]
    h_s[...] = h.astype(BF16)

    c1 = math.sqrt(2.0 / math.pi)
    for j in range(N_FF_TILES):
        cols = slice(j * tw, (j + 1) * tw)
        sl = j % 2
        u = _dot(h_s[...], wup_ref[:, cols])
        ub_s[sl, 0:SUBLANES, :] = carry_s[j]
        ub_s[sl, SUBLANES:SUBLANES + ts, :] = u
        carry_s[j] = u[ts - SUBLANES:ts, :]
        w = cw_ref[:, cols]
        y = (u * w[2:3] + ub_s[sl, SUBLANES - 1:SUBLANES - 1 + ts, :] * w[1:2]
             + ub_s[sl, SUBLANES - 2:SUBLANES - 2 + ts, :] * w[0:1] + cb_ref[:, cols])
        g = y[:, 0:FF_TILE]
        val = y[:, FF_TILE:tw]
        cdf = 0.5 * (1.0 + jnp.tanh(c1 * (g + 0.044715 * (g * g * g))))
        act_s[:, j * FF_TILE:(j + 1) * FF_TILE] = (g * cdf * val).astype(BF16)

    y2 = _dot(act_s[...], wdn_ref[...])
    o_ref[...] = x_ref[...] + mod_ref[5:6, :] * (_rms(y2) * gpost_ref[...])


def _ffn(x, mod, gpre, wup, cw, cb, wdn, gpost, ts):
    bsz, seq, d = x.shape
    tw = 2 * FF_TILE
    in_specs = [
        pl.BlockSpec((None, ts, d), lambda b, s: (b, s, 0)),
        pl.BlockSpec((None, N_MOD, d), lambda b, s: (b, 0, 0)),
        _const_spec((1, d)),
        _const_spec((d, 2 * D_FF)),
        _const_spec((FFN_CONV, 2 * D_FF)),
        _const_spec((1, 2 * D_FF)),
        _const_spec((D_FF, d)),
        _const_spec((1, d)),
    ]
    scratch = [
        pltpu.VMEM((ts, d), BF16),
        pltpu.VMEM((2, SUBLANES + ts, tw), F32),
        pltpu.VMEM((N_FF_TILES, SUBLANES, tw), F32),
        pltpu.VMEM((ts, D_FF), BF16),
    ]
    return pl.pallas_call(
        _ffn_kernel,
        grid=(bsz, seq // ts),
        in_specs=in_specs,
        out_specs=pl.BlockSpec((None, ts, d), lambda b, s: (b, s, 0)),
        out_shape=jax.ShapeDtypeStruct((bsz, seq, d), F32),
        scratch_shapes=scratch,
        compiler_params=pltpu.CompilerParams(
            dimension_semantics=("arbitrary", "arbitrary"), vmem_limit_bytes=VMEM_LIMIT),
        name="conv_ffn",
    )(x, mod, gpre, wup, cw, cb, wdn, gpost)


def _rope_tables(seq):
    half = ATT_HD // 2
    inv = ROPE_THETA ** (-jnp.arange(half, dtype=F32) / half)
    ang = jnp.arange(seq).astype(F32)[:, None] * inv[None, :]
    cos = jnp.cos(ang)
    sin = jnp.sin(ang)
    return jnp.tile(cos, (1, 4)), jnp.concatenate([-sin, -sin, sin, sin], axis=1)


def kernel(x, c, w_ada, b_ada, pre_mix_g, w_in, ml_conv_w, ml_conv_b, ml_i_b, ml_f_b, ml_norm_g,
           attn_sinks, attn_norm_g, w_out, post_mix_g, pre_ffn_g, w_up, ffn_conv_w, ffn_conv_b,
           w_down, post_ffn_g):
    bsz, seq, d = x.shape
    depth = w_ada.shape[0]
    ts = min(512, seq)
    in_perm = _in_proj_perm()
    up_perm = _up_proj_perm()
    cos, sin = _rope_tables(seq)
    for l in range(depth):
        mod = _adaln(c, w_ada[l], b_ada[l]).reshape(bsz, N_MOD, d)
        win = jnp.concatenate([w_in[l], jnp.zeros((d, 1), F32)], axis=1)[:, in_perm].astype(BF16)
        gb = jnp.concatenate([ml_i_b[l], ml_f_b[l], jnp.zeros((LANES - 2 * ML_H,), F32)]).reshape(1, LANES)
        sinks = jnp.broadcast_to(attn_sinks[l][:, None], (ATT_H, LANES))
        x = _mixer(x, mod, pre_mix_g[l].reshape(1, d), win, ml_conv_w[l], ml_conv_b[l].reshape(1, -1),
                   gb, ml_norm_g[l].reshape(1, ML_W), sinks, attn_norm_g[l].reshape(1, ATT_W),
                   cos, sin, w_out[l].astype(BF16), post_mix_g[l].reshape(1, d), ts)
        x = _ffn(x, mod, pre_ffn_g[l].reshape(1, d), w_up[l][:, up_perm].astype(BF16),
                 ffn_conv_w[l][:, up_perm], ffn_conv_b[l][up_perm].reshape(1, -1),
                 w_down[l].astype(BF16), post_ffn_g[l].reshape(1, d), ts)
    return x
```

```python
import functools
import math

import numpy as np
import jax
import jax.numpy as jnp
from jax import lax
from jax.experimental import pallas as pl
from jax.experimental.pallas import tpu as pltpu

F32 = jnp.float32
BF16 = jnp.bfloat16

D_MODEL = 1024
ML_H = 4
ML_HD = 128
ML_W = ML_H * ML_HD
ML_CONV = 4
ATT_H = 8
ATT_KV = 2
ATT_HD = 64
ATT_W = ATT_H * ATT_HD
WINDOW = 128
ROPE_THETA = 10000.0
MIX_W = ML_W + ATT_W
IN_COLS = 4 * ML_W + 2 * ML_H + ATT_W + 2 * ATT_KV * ATT_HD
D_FF = 2816
FFN_CONV = 3
N_MOD = 6
RMS_EPS = 1e-6

LANES = 128
SUBLANES = 8
BLK = 128
NEG = -1e30

C_QK = 0
C_V = 2 * ML_W
C_O = 3 * ML_W
C_G = 4 * ML_W
C_QA = C_G + LANES
C_KA = C_QA + ATT_W
C_VA = C_KA + ATT_KV * LANES
NW = C_VA + ATT_KV * LANES

FF_TILE = 256
N_FF_TILES = D_FF // FF_TILE

VMEM_LIMIT = 56 * 1024 * 1024

MLSTM_BLOCKS_PER_STAGE = 2
ATT_BLOCKS_PER_STAGE = 1


def _relayout_in_proj(w):
    d = w.shape[0]
    half = ATT_HD // 2
    g0 = 4 * ML_W
    q0 = g0 + 2 * ML_H
    k0 = q0 + ATT_W
    v0 = k0 + ATT_KV * ATT_HD
    gates = jnp.pad(w[:, g0:q0], ((0, 0), (0, LANES - 2 * ML_H)))
    wq = w[:, q0:k0].reshape(d, ATT_H // 2, 2, 2, half).transpose(0, 1, 3, 2, 4).reshape(d, ATT_W)
    wk = jnp.broadcast_to(w[:, k0:v0].reshape(d, ATT_KV, 2, 1, half),
                          (d, ATT_KV, 2, 2, half)).reshape(d, ATT_KV * LANES)
    wv = jnp.broadcast_to(w[:, v0:].reshape(d, ATT_KV, 1, ATT_HD),
                          (d, ATT_KV, 2, ATT_HD)).reshape(d, ATT_KV * LANES)
    return jnp.concatenate([w[:, :g0], gates, wq, wk, wv], axis=1).astype(BF16)


def _interleave_ff(a):
    lead = a.shape[:-1]
    n = len(lead)
    a = a.reshape(lead + (2, N_FF_TILES, FF_TILE))
    a = jnp.swapaxes(a, n, n + 1)
    return a.reshape(lead + (2 * D_FF,))


def _sigmoid(v):
    return 1.0 / (1.0 + jnp.exp(-v))


def _dot(a, b):
    return jnp.dot(a, b, preferred_element_type=F32)


def _dot_nt(a, b):
    return lax.dot_general(a, b, (((1,), (1,)), ((), ())), preferred_element_type=F32)


def _dot_tn(a, b):
    return lax.dot_general(a, b, (((0,), (0,)), ((), ())), preferred_element_type=F32)


def _rms(v):
    return v * lax.rsqrt(jnp.mean(v * v, axis=-1, keepdims=True) + RMS_EPS)


def _adaln_kernel(c_ref, w_ref, b_ref, o_ref):
    c = c_ref[...]
    ca = (c * _sigmoid(c)).astype(BF16)
    o_ref[...] = _dot(ca, w_ref[...].astype(BF16)) + b_ref[...]


def _adaln(c, w, b):
    bsz, d = c.shape
    n = w.shape[1]
    tn = 1024
    return pl.pallas_call(
        _adaln_kernel,
        grid=(n // tn,),
        in_specs=[
            pl.BlockSpec((bsz, d), lambda j: (0, 0)),
            pl.BlockSpec((d, tn), lambda j: (0, j)),
            pl.BlockSpec((1, tn), lambda j: (0, j)),
        ],
        out_specs=pl.BlockSpec((bsz, tn), lambda j: (0, j)),
        out_shape=jax.ShapeDtypeStruct((bsz, n), F32),
        compiler_params=pltpu.CompilerParams(dimension_semantics=("arbitrary",)),
        name="adaln",
    )(c, w, b.reshape(1, n))


def _dup(v):
    return jnp.concatenate([v, v], axis=1)


def _mlstm_tile(q_s, kt_ml_s, v_s, og_s, bc_s, ipt_s, bt_s, mlg_ref, mix_s, st_s, m_s, nblk, tril, ones_b):
    for r0 in range(0, nblk, MLSTM_BLOCKS_PER_STAGE):
        units = [(r, j) for r in range(r0, min(r0 + MLSTM_BLOCKS_PER_STAGE, nblk)) for j in range(ML_H)]
        rows = {u: slice(u[0] * BLK, (u[0] + 1) * BLK) for u in units}
        hcol = {u: slice(u[1] * ML_HD, (u[1] + 1) * ML_HD) for u in units}
        a_row, b_last, g_row, gmax = {}, {}, {}, {}
        for u in units:
            r, j = u
            i_row = ipt_s[r, j:j + 1, :]
            b_row = bt_s[r, ML_H + j:ML_H + j + 1, :]
            a_row[u] = i_row - b_row
            b_last[u] = b_row[:, BLK - 1:BLK]
            g_row[u] = b_last[u] - b_row + i_row
            gmax[u] = jnp.max(g_row[u], axis=-1, keepdims=True)
        dma = {u: jnp.where(tril, a_row[u], NEG) for u in units}
        cm_col = {u: jnp.max(dma[u], axis=-1, keepdims=True) for u in units}
        cm = {u: jnp.broadcast_to(cm_col[u], (BLK, LANES)) for u in units}
        pw = {u: jnp.exp(dma[u] - cm[u]) for u in units}
        qk = {u: _dot(q_s[rows[u], hcol[u]], kt_ml_s[u[0], u[1]]) for u in units}
        nloc, aloc = {}, {}
        for u in units:
            v_aug = jnp.concatenate([v_s[rows[u], hcol[u]], ones_b], axis=1)
            nloc[u] = _dot((qk[u] * pw[u]).astype(BF16), v_aug)
        for u in units:
            v_aug = jnp.concatenate([v_s[rows[u], hcol[u]], ones_b], axis=1)
            kw = (kt_ml_s[u[0], u[1]].astype(F32) * jnp.exp(g_row[u] - gmax[u])).astype(BF16)
            aloc[u] = _dot(kw, v_aug)
        m_prev, st_b = {}, {}
        for j in range(ML_H):
            st = st_s[j]
            m = m_s[j:j + 1, :]
            for r in range(r0, min(r0 + MLSTM_BLOCKS_PER_STAGE, nblk)):
                u = (r, j)
                m_prev[u] = m
                st_b[u] = st.astype(BF16)
                m_new = jnp.maximum(b_last[u] + m, gmax[u])
                decay = jnp.exp(b_last[u] + m - m_new)
                gain = jnp.exp(gmax[u] - m_new)
                st = _dup(decay) * st + _dup(gain) * aloc[u]
                m = m_new
            st_s[j] = st
            m_s[j:j + 1, :] = m
        inter = {u: _dot(q_s[rows[u], hcol[u]], st_b[u]) for u in units}
        hv, mm = {}, {}
        for u in units:
            mm[u] = jnp.maximum(m_prev[u], cm[u])
            num = nloc[u] * _dup(jnp.exp(cm[u] - mm[u])) + _dup(jnp.exp(m_prev[u] - mm[u])) * inter[u]
            b_rep = jnp.broadcast_to(bc_s[u[0], :, ML_H + u[1]:ML_H + u[1] + 1], (BLK, LANES))
            floor = jnp.exp(-(b_rep + mm[u]))
            hv[u] = num[:, 0:ML_HD] / jnp.maximum(jnp.abs(num[:, ML_HD:2 * ML_HD]), floor)
        ms = {u: _dot((hv[u] * hv[u]).astype(BF16), ones_b) * (1.0 / ML_HD) for u in units}
        for u in units:
            hn = hv[u] * lax.rsqrt(ms[u] + RMS_EPS) * mlg_ref[:, hcol[u]]
            mix_s[rows[u], hcol[u]] = (hn * _sigmoid(og_s[rows[u], hcol[u]])).astype(BF16)


def _attention_tile(qa_s, kt_s, vt_s, sink_ref, atg_ref, mix_s, nblk, has_prev_tile, lane, ones_b):
    head_a = (lane % (LANES // 2)) < (LANES // 4)
    lane_lo = lane < (LANES // 2)
    rows4 = lax.broadcasted_iota(jnp.int32, (4 * BLK, 2 * BLK), 0) % BLK
    cols4 = lax.broadcasted_iota(jnp.int32, (4 * BLK, 2 * BLK), 1)
    own = cols4 >= BLK
    band_own = own & ((cols4 - BLK) <= rows4)
    band_prev = (~own) & (cols4 > rows4)
    sink4 = {}
    for g in range(ATT_KV):
        sink4[g] = jnp.concatenate(
            [jnp.broadcast_to(sink_ref[h:h + 1, :], (BLK, LANES)) for h in range(4 * g, 4 * g + 4)], axis=0)
    for r0 in range(0, nblk, ATT_BLOCKS_PER_STAGE):
        blocks = range(r0, min(r0 + ATT_BLOCKS_PER_STAGE, nblk))
        units = [(r, g) for r in blocks for g in range(ATT_KV)]
        sc = {}
        for u in units:
            r, g = u
            rows = slice(r * BLK, (r + 1) * BLK)
            qs = []
            for p in (2 * g, 2 * g + 1):
                qp = qa_s[rows, p * LANES:(p + 1) * LANES]
                qs.append(jnp.where(head_a, qp, jnp.zeros_like(qp)))
                qs.append(jnp.where(head_a, jnp.zeros_like(qp), qp))
            kb = kt_s[g, r * BLK:(r + 2) * BLK, :]
            valid = band_own | ((band_prev & has_prev_tile) if r == 0 else band_prev)
            sc[u] = jnp.where(valid, _dot_nt(jnp.concatenate(qs, axis=0), kb), NEG)
        m_col = {u: jnp.max(sc[u], axis=-1, keepdims=True) for u in units}
        m = {u: jnp.maximum(jnp.broadcast_to(m_col[u], (4 * BLK, LANES)), sink4[u[1]]) for u in units}
        pe = {u: jnp.exp(sc[u] - _dup(m[u])).astype(BF16) for u in units}
        o4a = {u: _dot(pe[u], vt_s[u[1], u[0] * BLK:(u[0] + 2) * BLK, :]) for u in units}
        for r in blocks:
            rows = slice(r * BLK, (r + 1) * BLK)
            pair_out = []
            for g in range(ATT_KV):
                u = (r, g)
                o4 = o4a[u][:, 0:LANES] / (o4a[u][:, LANES:2 * LANES] + jnp.exp(sink4[g] - m[u]))
                for pp in range(2):
                    pair_out.append(jnp.where(lane_lo, o4[(2 * pp) * BLK:(2 * pp + 1) * BLK, :],
                                              o4[(2 * pp + 1) * BLK:(2 * pp + 2) * BLK, :]))
            sq = sum(po * po for po in pair_out)
            rs = lax.rsqrt(_dot(sq.astype(BF16), ones_b) * (1.0 / ATT_W) + RMS_EPS)
            for p in range(ATT_H // 2):
                mix_s[rows, ML_W + p * LANES:ML_W + (p + 1) * LANES] = (
                    pair_out[p] * rs * atg_ref[:, p * LANES:(p + 1) * LANES]).astype(BF16)


def _mixer_kernel(x_ref, mod_ref, gpre_ref, win_ref, cw_ref, cb_ref, gb_ref, mlg_ref, sink_ref,
                  atg_ref, cos_ref, sin_ref, wout_ref, gpost_ref, o_ref,
                  h_s, xp_s, q_s, kt_ml_s, v_s, og_s, bc_s, ipt_s, bt_s, qa_s, kt_s, vt_s,
                  mix_s, st_s, m_s):
    s_idx = pl.program_id(1)
    ts = x_ref.shape[0]
    nblk = ts // BLK

    @pl.when(s_idx == 0)
    def _reset_state():
        xp_s[0:SUBLANES, :] = jnp.zeros((SUBLANES, 2 * ML_W), F32)
        kt_s[:, 0:BLK, :] = jnp.zeros((ATT_KV, BLK, LANES), BF16)
        vt_s[:, :, 0:LANES] = jnp.zeros((ATT_KV, BLK + ts, LANES), BF16)
        vt_s[:, :, LANES:2 * LANES] = jnp.ones((ATT_KV, BLK + ts, LANES), BF16)
        st_s[...] = jnp.zeros(st_s.shape, F32)
        m_s[...] = jnp.zeros(m_s.shape, F32)

    @pl.when(s_idx > 0)
    def _carry_kv():
        for g in range(ATT_KV):
            kt_s[g, 0:BLK, :] = kt_s[g, ts:ts + BLK, :]
            vt_s[g, 0:BLK, 0:LANES] = vt_s[g, ts:ts + BLK, 0:LANES]

    x = x_ref[...]
    h = (_rms(x) * gpre_ref[...]) * (1.0 + mod_ref[1:2, :]) + mod_ref[0:1, :]
    h_s[...] = h.astype(BF16)

    kscale = 1.0 / math.sqrt(ML_HD)
    cchunk = 256
    for cj in range(2 * ML_W // cchunk):
        cols = slice(cj * cchunk, (cj + 1) * cchunk)
        u = _dot(h_s[...], win_ref[:, C_QK + cj * cchunk:C_QK + (cj + 1) * cchunk])
        xp_s[SUBLANES:SUBLANES + ts, cols] = u
        w = cw_ref[:, cols]
        y = (u * w[3:4] + xp_s[SUBLANES - 1:SUBLANES - 1 + ts, cols] * w[2:3]
             + xp_s[SUBLANES - 2:SUBLANES - 2 + ts, cols] * w[1:2]
             + xp_s[SUBLANES - 3:SUBLANES - 3 + ts, cols] * w[0:1] + cb_ref[:, cols])
        a = y * _sigmoid(y)
        if cj * cchunk < ML_W:
            q_s[:, cols] = a.astype(BF16)
        else:
            ak = a * kscale
            for jj in range(cchunk // ML_HD):
                j = (cj * cchunk - ML_W) // ML_HD + jj
                for r in range(nblk):
                    kt_ml_s[r, j] = ak[r * BLK:(r + 1) * BLK, jj * ML_HD:(jj + 1) * ML_HD].T.astype(BF16)
        xp_s[0:SUBLANES, cols] = u[ts - SUBLANES:ts, :]

    v_s[...] = _dot(h_s[...], win_ref[:, C_V:C_V + ML_W]).astype(BF16)
    og_s[...] = _dot(h_s[...], win_ref[:, C_O:C_O + ML_W])

    gts = _dot(h_s[...], win_ref[:, C_G:C_G + LANES]) + gb_ref[...]
    lsg = jnp.minimum(gts, 0.0) - jnp.log(1.0 + jnp.exp(-jnp.abs(gts)))
    row_i = lax.broadcasted_iota(jnp.int32, (BLK, BLK), 0)
    col_i = lax.broadcasted_iota(jnp.int32, (BLK, BLK), 1)
    tril = row_i >= col_i
    tril_b = jnp.where(tril, 1.0, 0.0).astype(BF16)
    for r in range(nblk):
        ipr = gts[r * BLK:(r + 1) * BLK, :]
        lsr = lsg[r * BLK:(r + 1) * BLK, :]
        hi = lsr.astype(BF16)
        r1 = lsr - hi.astype(F32)
        mid = r1.astype(BF16)
        lo = (r1 - mid.astype(F32)).astype(BF16)
        bc = _dot(tril_b, hi) + _dot(tril_b, mid) + _dot(tril_b, lo)
        bc_s[r] = bc
        ipt_s[r] = ipr.T[0:SUBLANES, :]
        bt_s[r] = bc.T[0:SUBLANES, :]

    cos = cos_ref[...]
    sin = sin_ref[...]
    qat = _dot(h_s[...], win_ref[:, C_QA:C_QA + ATT_W])
    qscale = 1.0 / math.sqrt(ATT_HD)
    for p in range(ATT_H // 2):
        qp = qat[:, p * LANES:(p + 1) * LANES]
        qr = (qp * cos + pltpu.roll(qp, LANES // 2, 1) * sin) * qscale
        qa_s[:, p * LANES:(p + 1) * LANES] = qr.astype(BF16)
    kat = _dot(h_s[...], win_ref[:, C_KA:C_KA + ATT_KV * LANES])
    vat = _dot(h_s[...], win_ref[:, C_VA:C_VA + ATT_KV * LANES])
    for g in range(ATT_KV):
        kp = kat[:, g * LANES:(g + 1) * LANES]
        kr = kp * cos + pltpu.roll(kp, LANES // 2, 1) * sin
        kt_s[g, BLK:BLK + ts, :] = kr.astype(BF16)
        vt_s[g, BLK:BLK + ts, 0:LANES] = vat[:, g * LANES:(g + 1) * LANES].astype(BF16)

    lane = lax.broadcasted_iota(jnp.int32, (BLK, LANES), 1)
    ones_b = jnp.ones((BLK, LANES), BF16)
    _mlstm_tile(q_s, kt_ml_s, v_s, og_s, bc_s, ipt_s, bt_s, mlg_ref, mix_s, st_s, m_s, nblk, tril, ones_b)
    _attention_tile(qa_s, kt_s, vt_s, sink_ref, atg_ref, mix_s, nblk, s_idx > 0, lane, ones_b)

    y = _dot(mix_s[...], wout_ref[...])
    o_ref[...] = x_ref[...] + mod_ref[2:3, :] * (_rms(y) * gpost_ref[...])


def _const_spec(shape):
    nd = len(shape)
    return pl.BlockSpec(shape, lambda b, s: (0,) * nd, pipeline_mode=pl.Buffered(1))


def _mixer(x, mod, gpre, win, cw, cb, gb, mlg, sinks, atg, cos, sin, wout, gpost, ts):
    bsz, seq, d = x.shape
    nblk = ts // BLK
    in_specs = [
        pl.BlockSpec((None, ts, d), lambda b, s: (b, s, 0)),
        pl.BlockSpec((None, N_MOD, d), lambda b, s: (b, 0, 0)),
        _const_spec((1, d)),
        _const_spec((d, NW)),
        _const_spec((ML_CONV, 2 * ML_W)),
        _const_spec((1, 2 * ML_W)),
        _const_spec((1, LANES)),
        _const_spec((1, ML_W)),
        _const_spec((ATT_H, LANES)),
        _const_spec((1, ATT_W)),
        pl.BlockSpec((ts, LANES), lambda b, s: (s, 0)),
        pl.BlockSpec((ts, LANES), lambda b, s: (s, 0)),
        _const_spec((MIX_W, d)),
        _const_spec((1, d)),
    ]
    scratch = [
        pltpu.VMEM((ts, d), BF16),
        pltpu.VMEM((SUBLANES + ts, 2 * ML_W), F32),
        pltpu.VMEM((ts, ML_W), BF16),
        pltpu.VMEM((nblk, ML_H, ML_HD, BLK), BF16),
        pltpu.VMEM((ts, ML_W), BF16),
        pltpu.VMEM((ts, ML_W), F32),
        pltpu.VMEM((nblk, BLK, LANES), F32),
        pltpu.VMEM((nblk, SUBLANES, BLK), F32),
        pltpu.VMEM((nblk, SUBLANES, BLK), F32),
        pltpu.VMEM((ts, ATT_W), BF16),
        pltpu.VMEM((ATT_KV, BLK + ts, LANES), BF16),
        pltpu.VMEM((ATT_KV, BLK + ts, 2 * LANES), BF16),
        pltpu.VMEM((ts, MIX_W), BF16),
        pltpu.VMEM((ML_H, ML_HD, 2 * ML_HD), F32),
        pltpu.VMEM((SUBLANES, LANES), F32),
    ]
    return pl.pallas_call(
        _mixer_kernel,
        grid=(bsz, seq // ts),
        in_specs=in_specs,
        out_specs=pl.BlockSpec((None, ts, d), lambda b, s: (b, s, 0)),
        out_shape=jax.ShapeDtypeStruct((bsz, seq, d), F32),
        scratch_shapes=scratch,
        compiler_params=pltpu.CompilerParams(
            dimension_semantics=("arbitrary", "arbitrary"), vmem_limit_bytes=VMEM_LIMIT),
        name="mixer",
    )(x, mod, gpre, win, cw, cb, gb, mlg, sinks, atg, cos, sin, wout, gpost)


def _ffn_kernel(x_ref, mod_ref, gpre_ref, wup_ref, cw_ref, cb_ref, wdn_ref, gpost_ref, o_ref,
                h_s, ub_s, carry_s, act_s):
    s_idx = pl.program_id(1)
    ts = x_ref.shape[0]
    tw = 2 * FF_TILE

    @pl.when(s_idx == 0)
    def _reset_state():
        carry_s[...] = jnp.zeros(carry_s.shape, F32)

    x = x_ref[...]
    h = (_rms(x) * gpre_ref[...]) * (1.0 + mod_ref[4:5, :]) + mod_ref[3:4, :]
    h_s[...] = h.astype(BF16)

    c1 = math.sqrt(2.0 / math.pi)
    for j in range(N_FF_TILES):
        cols = slice(j * tw, (j + 1) * tw)
        sl = j % 2
        u = _dot(h_s[...], wup_ref[:, cols])
        ub_s[sl, 0:SUBLANES, :] = carry_s[j]
        ub_s[sl, SUBLANES:SUBLANES + ts, :] = u
        carry_s[j] = u[ts - SUBLANES:ts, :]
        w = cw_ref[:, cols]
        y = (u * w[2:3] + ub_s[sl, SUBLANES - 1:SUBLANES - 1 + ts, :] * w[1:2]
             + ub_s[sl, SUBLANES - 2:SUBLANES - 2 + ts, :] * w[0:1] + cb_ref[:, cols])
        g = y[:, 0:FF_TILE]
        val = y[:, FF_TILE:tw]
        cdf = 0.5 * (1.0 + jnp.tanh(c1 * (g + 0.044715 * (g * g * g))))
        act_s[:, j * FF_TILE:(j + 1) * FF_TILE] = (g * cdf * val).astype(BF16)

    y2 = _dot(act_s[...], wdn_ref[...])
    o_ref[...] = x_ref[...] + mod_ref[5:6, :] * (_rms(y2) * gpost_ref[...])


def _ffn(x, mod, gpre, wup, cw, cb, wdn, gpost, ts):
    bsz, seq, d = x.shape
    tw = 2 * FF_TILE
    in_specs = [
        pl.BlockSpec((None, ts, d), lambda b, s: (b, s, 0)),
        pl.BlockSpec((None, N_MOD, d), lambda b, s: (b, 0, 0)),
        _const_spec((1, d)),
        _const_spec((d, 2 * D_FF)),
        _const_spec((FFN_CONV, 2 * D_FF)),
        _const_spec((1, 2 * D_FF)),
        _const_spec((D_FF, d)),
        _const_spec((1, d)),
    ]
    scratch = [
        pltpu.VMEM((ts, d), BF16),
        pltpu.VMEM((2, SUBLANES + ts, tw), F32),
        pltpu.VMEM((N_FF_TILES, SUBLANES, tw), F32),
        pltpu.VMEM((ts, D_FF), BF16),
    ]
    return pl.pallas_call(
        _ffn_kernel,
        grid=(bsz, seq // ts),
        in_specs=in_specs,
        out_specs=pl.BlockSpec((None, ts, d), lambda b, s: (b, s, 0)),
        out_shape=jax.ShapeDtypeStruct((bsz, seq, d), F32),
        scratch_shapes=scratch,
        compiler_params=pltpu.CompilerParams(
            dimension_semantics=("arbitrary", "arbitrary"), vmem_limit_bytes=VMEM_LIMIT),
        name="conv_ffn",
    )(x, mod, gpre, wup, cw, cb, wdn, gpost)


def _rope_tables(seq):
    half = ATT_HD // 2
    inv = ROPE_THETA ** (-jnp.arange(half, dtype=F32) / half)
    ang = jnp.arange(seq).astype(F32)[:, None] * inv[None, :]
    cos = jnp.cos(ang)
    sin = jnp.sin(ang)
    return jnp.tile(cos, (1, 4)), jnp.concatenate([-sin, -sin, sin, sin], axis=1)


def kernel(x, c, w_ada, b_ada, pre_mix_g, w_in, ml_conv_w, ml_conv_b, ml_i_b, ml_f_b, ml_norm_g,
           attn_sinks, attn_norm_g, w_out, post_mix_g, pre_ffn_g, w_up, ffn_conv_w, ffn_conv_b,
           w_down, post_ffn_g):
    bsz, seq, d = x.shape
    depth = w_ada.shape[0]
    ts = min(512, seq)
    cos, sin = _rope_tables(seq)
    for l in range(depth):
        mod = _adaln(c, w_ada[l], b_ada[l]).reshape(bsz, N_MOD, d)
        win = _relayout_in_proj(w_in[l])
        gb = jnp.concatenate([ml_i_b[l], ml_f_b[l], jnp.zeros((LANES - 2 * ML_H,), F32)]).reshape(1, LANES)
        sinks = jnp.broadcast_to(attn_sinks[l][:, None], (ATT_H, LANES))
        x = _mixer(x, mod, pre_mix_g[l].reshape(1, d), win, ml_conv_w[l], ml_conv_b[l].reshape(1, -1),
                   gb, ml_norm_g[l].reshape(1, ML_W), sinks, attn_norm_g[l].reshape(1, ATT_W),
                   cos, sin, w_out[l].astype(BF16), post_mix_g[l].reshape(1, d), ts)
        x = _ffn(x, mod, pre_ffn_g[l].reshape(1, d), _interleave_ff(w_up[l]).astype(BF16),
                 _interleave_ff(ffn_conv_w[l]), _interleave_ff(ffn_conv_b[l]).reshape(1, -1),
                 w_down[l].astype(BF16), post_ffn_g[l].reshape(1, d), ts)
    return x
```

```python
import functools
import math

import numpy as np
import jax
import jax.numpy as jnp
from jax import lax
from jax.experimental import pallas as pl
from jax.experimental.pallas import tpu as pltpu

F32 = jnp.float32
BF16 = jnp.bfloat16

D_MODEL = 1024
ML_H = 4
ML_HD = 128
ML_W = ML_H * ML_HD
ML_CONV = 4
ATT_H = 8
ATT_KV = 2
ATT_HD = 64
ATT_W = ATT_H * ATT_HD
WINDOW = 128
ROPE_THETA = 10000.0
MIX_W = ML_W + ATT_W
IN_COLS = 4 * ML_W + 2 * ML_H + ATT_W + 2 * ATT_KV * ATT_HD
D_FF = 2816
FFN_CONV = 3
N_MOD = 6
RMS_EPS = 1e-6

LANES = 128
SUBLANES = 8
BLK = 128
NEG = -1e30

C_QK = 0
C_V = 2 * ML_W
C_O = 3 * ML_W
C_G = 4 * ML_W
C_QA = C_G + LANES
C_KA = C_QA + ATT_W
C_VA = C_KA + ATT_KV * LANES
NW = C_VA + ATT_KV * LANES

FF_TILE = 256
N_FF_TILES = D_FF // FF_TILE

VMEM_LIMIT = 56 * 1024 * 1024

MLSTM_BLOCKS_PER_STAGE = 2
ATT_BLOCKS_PER_STAGE = 1


def _relayout_in_proj(w):
    d = w.shape[0]
    half = ATT_HD // 2
    g0 = 4 * ML_W
    q0 = g0 + 2 * ML_H
    k0 = q0 + ATT_W
    v0 = k0 + ATT_KV * ATT_HD
    gates = jnp.pad(w[:, g0:q0], ((0, 0), (0, LANES - 2 * ML_H)))
    wq = w[:, q0:k0].reshape(d, ATT_H // 2, 2, 2, half).transpose(0, 1, 3, 2, 4).reshape(d, ATT_W)
    wk = jnp.broadcast_to(w[:, k0:v0].reshape(d, ATT_KV, 2, 1, half),
                          (d, ATT_KV, 2, 2, half)).reshape(d, ATT_KV * LANES)
    wv = jnp.broadcast_to(w[:, v0:].reshape(d, ATT_KV, 1, ATT_HD),
                          (d, ATT_KV, 2, ATT_HD)).reshape(d, ATT_KV * LANES)
    return jnp.concatenate([w[:, :g0], gates, wq, wk, wv], axis=1).astype(BF16)


def _sigmoid(v):
    return 1.0 / (1.0 + jnp.exp(-v))


def _dot(a, b):
    return jnp.dot(a, b, preferred_element_type=F32)


def _dot_nt(a, b):
    return lax.dot_general(a, b, (((1,), (1,)), ((), ())), preferred_element_type=F32)


def _dot_tn(a, b):
    return lax.dot_general(a, b, (((0,), (0,)), ((), ())), preferred_element_type=F32)


def _rms(v):
    return v * lax.rsqrt(jnp.mean(v * v, axis=-1, keepdims=True) + RMS_EPS)


def _adaln_kernel(c_ref, w_ref, b_ref, o_ref):
    c = c_ref[...]
    ca = (c * _sigmoid(c)).astype(BF16)
    o_ref[...] = _dot(ca, w_ref[...].astype(BF16)) + b_ref[...]


def _adaln(c, w, b):
    bsz, d = c.shape
    n = w.shape[1]
    tn = 1024
    return pl.pallas_call(
        _adaln_kernel,
        grid=(n // tn,),
        in_specs=[
            pl.BlockSpec((bsz, d), lambda j: (0, 0)),
            pl.BlockSpec((d, tn), lambda j: (0, j)),
            pl.BlockSpec((1, tn), lambda j: (0, j)),
        ],
        out_specs=pl.BlockSpec((bsz, tn), lambda j: (0, j)),
        out_shape=jax.ShapeDtypeStruct((bsz, n), F32),
        compiler_params=pltpu.CompilerParams(dimension_semantics=("arbitrary",)),
        name="adaln",
    )(c, w, b.reshape(1, n))


def _dup(v):
    return jnp.concatenate([v, v], axis=1)


def _mlstm_tile(q_s, kt_ml_s, v_s, og_s, bc_s, ipt_s, bt_s, mlg_ref, mix_s, st_s, m_s, nblk, tril, ones_b):
    for r0 in range(0, nblk, MLSTM_BLOCKS_PER_STAGE):
        units = [(r, j) for r in range(r0, min(r0 + MLSTM_BLOCKS_PER_STAGE, nblk)) for j in range(ML_H)]
        rows = {u: slice(u[0] * BLK, (u[0] + 1) * BLK) for u in units}
        hcol = {u: slice(u[1] * ML_HD, (u[1] + 1) * ML_HD) for u in units}
        a_row, b_last, g_row, gmax = {}, {}, {}, {}
        for u in units:
            r, j = u
            i_row = ipt_s[r, j:j + 1, :]
            b_row = bt_s[r, ML_H + j:ML_H + j + 1, :]
            a_row[u] = i_row - b_row
            b_last[u] = b_row[:, BLK - 1:BLK]
            g_row[u] = b_last[u] - b_row + i_row
            gmax[u] = jnp.max(g_row[u], axis=-1, keepdims=True)
        dma = {u: jnp.where(tril, a_row[u], NEG) for u in units}
        cm_col = {u: jnp.max(dma[u], axis=-1, keepdims=True) for u in units}
        cm = {u: jnp.broadcast_to(cm_col[u], (BLK, LANES)) for u in units}
        pw = {u: jnp.exp(dma[u] - cm[u]) for u in units}
        qk = {u: _dot(q_s[rows[u], hcol[u]], kt_ml_s[u[0], u[1]]) for u in units}
        nloc, aloc = {}, {}
        for u in units:
            v_aug = jnp.concatenate([v_s[rows[u], hcol[u]], ones_b], axis=1)
            nloc[u] = _dot((qk[u] * pw[u]).astype(BF16), v_aug)
        for u in units:
            v_aug = jnp.concatenate([v_s[rows[u], hcol[u]], ones_b], axis=1)
            kw = (kt_ml_s[u[0], u[1]].astype(F32) * jnp.exp(g_row[u] - gmax[u])).astype(BF16)
            aloc[u] = _dot(kw, v_aug)
        m_prev, st_b = {}, {}
        for j in range(ML_H):
            st = st_s[j]
            m = m_s[j:j + 1, :]
            for r in range(r0, min(r0 + MLSTM_BLOCKS_PER_STAGE, nblk)):
                u = (r, j)
                m_prev[u] = m
                st_b[u] = st.astype(BF16)
                m_new = jnp.maximum(b_last[u] + m, gmax[u])
                decay = jnp.exp(b_last[u] + m - m_new)
                gain = jnp.exp(gmax[u] - m_new)
                st = _dup(decay) * st + _dup(gain) * aloc[u]
                m = m_new
            st_s[j] = st
            m_s[j:j + 1, :] = m
        inter = {u: _dot(q_s[rows[u], hcol[u]], st_b[u]) for u in units}
        hv, mm = {}, {}
        for u in units:
            mm[u] = jnp.maximum(m_prev[u], cm[u])
            num = nloc[u] * _dup(jnp.exp(cm[u] - mm[u])) + _dup(jnp.exp(m_prev[u] - mm[u])) * inter[u]
            b_rep = jnp.broadcast_to(bc_s[u[0], :, ML_H + u[1]:ML_H + u[1] + 1], (BLK, LANES))
            floor = jnp.exp(-(b_rep + mm[u]))
            hv[u] = num[:, 0:ML_HD] / jnp.maximum(jnp.abs(num[:, ML_HD:2 * ML_HD]), floor)
        ms = {u: _dot((hv[u] * hv[u]).astype(BF16), ones_b) * (1.0 / ML_HD) for u in units}
        for u in units:
            hn = hv[u] * lax.rsqrt(ms[u] + RMS_EPS) * mlg_ref[:, hcol[u]]
            mix_s[rows[u], hcol[u]] = (hn * _sigmoid(og_s[rows[u], hcol[u]])).astype(BF16)


def _attention_tile(qa_s, kt_s, vt_s, sink_ref, atg_ref, mix_s, nblk, has_prev_tile, lane, ones_b):
    head_a = (lane % (LANES // 2)) < (LANES // 4)
    lane_lo = lane < (LANES // 2)
    rows4 = lax.broadcasted_iota(jnp.int32, (4 * BLK, 2 * BLK), 0) % BLK
    cols4 = lax.broadcasted_iota(jnp.int32, (4 * BLK, 2 * BLK), 1)
    own = cols4 >= BLK
    band_own = own & ((cols4 - BLK) <= rows4)
    band_prev = (~own) & (cols4 > rows4)
    sink4 = {}
    for g in range(ATT_KV):
        sink4[g] = jnp.concatenate(
            [jnp.broadcast_to(sink_ref[h:h + 1, :], (BLK, LANES)) for h in range(4 * g, 4 * g + 4)], axis=0)
    for r0 in range(0, nblk, ATT_BLOCKS_PER_STAGE):
        blocks = range(r0, min(r0 + ATT_BLOCKS_PER_STAGE, nblk))
        units = [(r, g) for r in blocks for g in range(ATT_KV)]
        sc = {}
        for u in units:
            r, g = u
            rows = slice(r * BLK, (r + 1) * BLK)
            qs = []
            for p in (2 * g, 2 * g + 1):
                qp = qa_s[rows, p * LANES:(p + 1) * LANES]
                qs.append(jnp.where(head_a, qp, jnp.zeros_like(qp)))
                qs.append(jnp.where(head_a, jnp.zeros_like(qp), qp))
            kb = kt_s[g, r * BLK:(r + 2) * BLK, :]
            valid = band_own | ((band_prev & has_prev_tile) if r == 0 else band_prev)
            sc[u] = jnp.where(valid, _dot_nt(jnp.concatenate(qs, axis=0), kb), NEG)
        m_col = {u: jnp.max(sc[u], axis=-1, keepdims=True) for u in units}
        m = {u: jnp.maximum(jnp.broadcast_to(m_col[u], (4 * BLK, LANES)), sink4[u[1]]) for u in units}
        pe = {u: jnp.exp(sc[u] - _dup(m[u])).astype(BF16) for u in units}
        o4a = {u: _dot(pe[u], vt_s[u[1], u[0] * BLK:(u[0] + 2) * BLK, :]) for u in units}
        for r in blocks:
            rows = slice(r * BLK, (r + 1) * BLK)
            pair_out = []
            for g in range(ATT_KV):
                u = (r, g)
                o4 = o4a[u][:, 0:LANES] / (o4a[u][:, LANES:2 * LANES] + jnp.exp(sink4[g] - m[u]))
                for pp in range(2):
                    pair_out.append(jnp.where(lane_lo, o4[(2 * pp) * BLK:(2 * pp + 1) * BLK, :],
                                              o4[(2 * pp + 1) * BLK:(2 * pp + 2) * BLK, :]))
            sq = sum(po * po for po in pair_out)
            rs = lax.rsqrt(_dot(sq.astype(BF16), ones_b) * (1.0 / ATT_W) + RMS_EPS)
            for p in range(ATT_H // 2):
                mix_s[rows, ML_W + p * LANES:ML_W + (p + 1) * LANES] = (
                    pair_out[p] * rs * atg_ref[:, p * LANES:(p + 1) * LANES]).astype(BF16)


def _mixer_kernel(x_ref, mod_ref, gpre_ref, win_ref, cw_ref, cb_ref, gb_ref, mlg_ref, sink_ref,
                  atg_ref, cos_ref, sin_ref, wout_ref, gpost_ref, o_ref,
                  h_s, xp_s, q_s, kt_ml_s, v_s, og_s, bc_s, ipt_s, bt_s, qa_s, kt_s, vt_s,
                  mix_s, st_s, m_s):
    s_idx = pl.program_id(1)
    ts = x_ref.shape[0]
    nblk = ts // BLK

    @pl.when(s_idx == 0)
    def _reset_state():
        xp_s[0:SUBLANES, :] = jnp.zeros((SUBLANES, 2 * ML_W), F32)
        kt_s[:, 0:BLK, :] = jnp.zeros((ATT_KV, BLK, LANES), BF16)
        vt_s[:, :, 0:LANES] = jnp.zeros((ATT_KV, BLK + ts, LANES), BF16)
        vt_s[:, :, LANES:2 * LANES] = jnp.ones((ATT_KV, BLK + ts, LANES), BF16)
        st_s[...] = jnp.zeros(st_s.shape, F32)
        m_s[...] = jnp.zeros(m_s.shape, F32)

    @pl.when(s_idx > 0)
    def _carry_kv():
        for g in range(ATT_KV):
            kt_s[g, 0:BLK, :] = kt_s[g, ts:ts + BLK, :]
            vt_s[g, 0:BLK, 0:LANES] = vt_s[g, ts:ts + BLK, 0:LANES]

    x = x_ref[...]
    h = (_rms(x) * gpre_ref[...]) * (1.0 + mod_ref[1:2, :]) + mod_ref[0:1, :]
    h_s[...] = h.astype(BF16)

    kscale = 1.0 / math.sqrt(ML_HD)
    cchunk = 256
    for cj in range(2 * ML_W // cchunk):
        cols = slice(cj * cchunk, (cj + 1) * cchunk)
        u = _dot(h_s[...], win_ref[:, C_QK + cj * cchunk:C_QK + (cj + 1) * cchunk])
        xp_s[SUBLANES:SUBLANES + ts, cols] = u
        w = cw_ref[:, cols]
        y = (u * w[3:4] + xp_s[SUBLANES - 1:SUBLANES - 1 + ts, cols] * w[2:3]
             + xp_s[SUBLANES - 2:SUBLANES - 2 + ts, cols] * w[1:2]
             + xp_s[SUBLANES - 3:SUBLANES - 3 + ts, cols] * w[0:1] + cb_ref[:, cols])
        a = y * _sigmoid(y)
        if cj * cchunk < ML_W:
            q_s[:, cols] = a.astype(BF16)
        else:
            ak = a * kscale
            for jj in range(cchunk // ML_HD):
                j = (cj * cchunk - ML_W) // ML_HD + jj
                for r in range(nblk):
                    kt_ml_s[r, j] = ak[r * BLK:(r + 1) * BLK, jj * ML_HD:(jj + 1) * ML_HD].T.astype(BF16)
        xp_s[0:SUBLANES, cols] = u[ts - SUBLANES:ts, :]

    v_s[...] = _dot(h_s[...], win_ref[:, C_V:C_V + ML_W]).astype(BF16)
    og_s[...] = _dot(h_s[...], win_ref[:, C_O:C_O + ML_W])

    gts = _dot(h_s[...], win_ref[:, C_G:C_G + LANES]) + gb_ref[...]
    lsg = jnp.minimum(gts, 0.0) - jnp.log(1.0 + jnp.exp(-jnp.abs(gts)))
    row_i = lax.broadcasted_iota(jnp.int32, (BLK, BLK), 0)
    col_i = lax.broadcasted_iota(jnp.int32, (BLK, BLK), 1)
    tril = row_i >= col_i
    tril_b = jnp.where(tril, 1.0, 0.0).astype(BF16)
    for r in range(nblk):
        ipr = gts[r * BLK:(r + 1) * BLK, :]
        lsr = lsg[r * BLK:(r + 1) * BLK, :]
        hi = lsr.astype(BF16)
        r1 = lsr - hi.astype(F32)
        mid = r1.astype(BF16)
        lo = (r1 - mid.astype(F32)).astype(BF16)
        bc = _dot(tril_b, hi) + _dot(tril_b, mid) + _dot(tril_b, lo)
        bc_s[r] = bc
        ipt_s[r] = ipr.T[0:SUBLANES, :]
        bt_s[r] = bc.T[0:SUBLANES, :]

    cos = cos_ref[...]
    sin = sin_ref[...]
    qat = _dot(h_s[...], win_ref[:, C_QA:C_QA + ATT_W])
    qscale = 1.0 / math.sqrt(ATT_HD)
    for p in range(ATT_H // 2):
        qp = qat[:, p * LANES:(p + 1) * LANES]
        qr = (qp * cos + pltpu.roll(qp, LANES // 2, 1) * sin) * qscale
        qa_s[:, p * LANES:(p + 1) * LANES] = qr.astype(BF16)
    kat = _dot(h_s[...], win_ref[:, C_KA:C_KA + ATT_KV * LANES])
    vat = _dot(h_s[...], win_ref[:, C_VA:C_VA + ATT_KV * LANES])
    for g in range(ATT_KV):
        kp = kat[:, g * LANES:(g + 1) * LANES]
        kr = kp * cos + pltpu.roll(kp, LANES // 2, 1) * sin
        kt_s[g, BLK:BLK + ts, :] = kr.astype(BF16)
        vt_s[g, BLK:BLK + ts, 0:LANES] = vat[:, g * LANES:(g + 1) * LANES].astype(BF16)

    lane = lax.broadcasted_iota(jnp.int32, (BLK, LANES), 1)
    ones_b = jnp.ones((BLK, LANES), BF16)
    _mlstm_tile(q_s, kt_ml_s, v_s, og_s, bc_s, ipt_s, bt_s, mlg_ref, mix_s, st_s, m_s, nblk, tril, ones_b)
    _attention_tile(qa_s, kt_s, vt_s, sink_ref, atg_ref, mix_s, nblk, s_idx > 0, lane, ones_b)

    y = _dot(mix_s[...], wout_ref[...])
    o_ref[...] = x_ref[...] + mod_ref[2:3, :] * (_rms(y) * gpost_ref[...])


def _const_spec(shape):
    nd = len(shape)
    return pl.BlockSpec(shape, lambda b, s: (0,) * nd, pipeline_mode=pl.Buffered(1))


def _mixer(x, mod, gpre, win, cw, cb, gb, mlg, sinks, atg, cos, sin, wout, gpost, ts):
    bsz, seq, d = x.shape
    nblk = ts // BLK
    in_specs = [
        pl.BlockSpec((None, ts, d), lambda b, s: (b, s, 0)),
        pl.BlockSpec((None, N_MOD, d), lambda b, s: (b, 0, 0)),
        _const_spec((1, d)),
        _const_spec((d, NW)),
        _const_spec((ML_CONV, 2 * ML_W)),
        _const_spec((1, 2 * ML_W)),
        _const_spec((1, LANES)),
        _const_spec((1, ML_W)),
        _const_spec((ATT_H, LANES)),
        _const_spec((1, ATT_W)),
        pl.BlockSpec((ts, LANES), lambda b, s: (s, 0)),
        pl.BlockSpec((ts, LANES), lambda b, s: (s, 0)),
        _const_spec((MIX_W, d)),
        _const_spec((1, d)),
    ]
    scratch = [
        pltpu.VMEM((ts, d), BF16),
        pltpu.VMEM((SUBLANES + ts, 2 * ML_W), F32),
        pltpu.VMEM((ts, ML_W), BF16),
        pltpu.VMEM((nblk, ML_H, ML_HD, BLK), BF16),
        pltpu.VMEM((ts, ML_W), BF16),
        pltpu.VMEM((ts, ML_W), F32),
        pltpu.VMEM((nblk, BLK, LANES), F32),
        pltpu.VMEM((nblk, SUBLANES, BLK), F32),
        pltpu.VMEM((nblk, SUBLANES, BLK), F32),
        pltpu.VMEM((ts, ATT_W), BF16),
        pltpu.VMEM((ATT_KV, BLK + ts, LANES), BF16),
        pltpu.VMEM((ATT_KV, BLK + ts, 2 * LANES), BF16),
        pltpu.VMEM((ts, MIX_W), BF16),
        pltpu.VMEM((ML_H, ML_HD, 2 * ML_HD), F32),
        pltpu.VMEM((SUBLANES, LANES), F32),
    ]
    return pl.pallas_call(
        _mixer_kernel,
        grid=(bsz, seq // ts),
        in_specs=in_specs,
        out_specs=pl.BlockSpec((None, ts, d), lambda b, s: (b, s, 0)),
        out_shape=jax.ShapeDtypeStruct((bsz, seq, d), F32),
        scratch_shapes=scratch,
        compiler_params=pltpu.CompilerParams(
            dimension_semantics=("arbitrary", "arbitrary"), vmem_limit_bytes=VMEM_LIMIT),
        name="mixer",
    )(x, mod, gpre, win, cw, cb, gb, mlg, sinks, atg, cos, sin, wout, gpost)


def _ffn_kernel(x_ref, mod_ref, gpre_ref, wup_ref, cw_ref, cb_ref, wdn_ref, gpost_ref, o_ref,
                h_s, ub_s, carry_s, act_s):
    s_idx = pl.program_id(1)
    ts = x_ref.shape[0]

    @pl.when(s_idx == 0)
    def _reset_state():
        carry_s[...] = jnp.zeros(carry_s.shape, F32)

    x = x_ref[...]
    h = (_rms(x) * gpre_ref[...]) * (1.0 + mod_ref[4:5, :]) + mod_ref[3:4, :]
    h_s[...] = h.astype(BF16)

    def up_proj(j):
        return tuple(_dot(h_s[...], wup_ref[:, half * D_FF + j * FF_TILE:half * D_FF + (j + 1) * FF_TILE])
                     for half in range(2))

    def causal_conv(u, j, half):
        cols = slice(half * D_FF + j * FF_TILE, half * D_FF + (j + 1) * FF_TILE)
        slot = 2 * (j % 2) + half
        ub_s[slot, 0:SUBLANES, :] = carry_s[j, half]
        ub_s[slot, SUBLANES:SUBLANES + ts, :] = u
        carry_s[j, half] = u[ts - SUBLANES:ts, :]
        w = cw_ref[:, cols]
        return (u * w[2:3] + ub_s[slot, SUBLANES - 1:SUBLANES - 1 + ts, :] * w[1:2]
                + ub_s[slot, SUBLANES - 2:SUBLANES - 2 + ts, :] * w[0:1] + cb_ref[:, cols])

    c1 = math.sqrt(2.0 / math.pi)
    c3 = c1 * 0.044715
    for j in range(N_FF_TILES):
        ug, uv = up_proj(j)
        g = causal_conv(ug, j, 0)
        half_val = causal_conv(uv, j, 1)
        th = jnp.tanh(g * (c3 * (g * g) + c1))
        gv = g * half_val
        act_s[:, j * FF_TILE:(j + 1) * FF_TILE] = (gv * th + gv).astype(BF16)

    y2 = _dot(act_s[...], wdn_ref[...])
    o_ref[...] = x_ref[...] + mod_ref[5:6, :] * (_rms(y2) * gpost_ref[...])


def _ffn(x, mod, gpre, wup, cw, cb, wdn, gpost, ts):
    bsz, seq, d = x.shape
    in_specs = [
        pl.BlockSpec((None, ts, d), lambda b, s: (b, s, 0)),
        pl.BlockSpec((None, N_MOD, d), lambda b, s: (b, 0, 0)),
        _const_spec((1, d)),
        _const_spec((d, 2 * D_FF)),
        _const_spec((FFN_CONV, 2 * D_FF)),
        _const_spec((1, 2 * D_FF)),
        _const_spec((D_FF, d)),
        _const_spec((1, d)),
    ]
    scratch = [
        pltpu.VMEM((ts, d), BF16),
        pltpu.VMEM((4, SUBLANES + ts, FF_TILE), F32),
        pltpu.VMEM((N_FF_TILES, 2, SUBLANES, FF_TILE), F32),
        pltpu.VMEM((ts, D_FF), BF16),
    ]
    return pl.pallas_call(
        _ffn_kernel,
        grid=(bsz, seq // ts),
        in_specs=in_specs,
        out_specs=pl.BlockSpec((None, ts, d), lambda b, s: (b, s, 0)),
        out_shape=jax.ShapeDtypeStruct((bsz, seq, d), F32),
        scratch_shapes=scratch,
        compiler_params=pltpu.CompilerParams(
            dimension_semantics=("arbitrary", "arbitrary"), vmem_limit_bytes=VMEM_LIMIT),
        name="conv_ffn",
    )(x, mod, gpre, wup, cw, cb, wdn, gpost)


def _rope_tables(seq):
    half = ATT_HD // 2
    inv = ROPE_THETA ** (-jnp.arange(half, dtype=F32) / half)
    ang = jnp.arange(seq).astype(F32)[:, None] * inv[None, :]
    cos = jnp.cos(ang)
    sin = jnp.sin(ang)
    return jnp.tile(cos, (1, 4)), jnp.concatenate([-sin, -sin, sin, sin], axis=1)


def kernel(x, c, w_ada, b_ada, pre_mix_g, w_in, ml_conv_w, ml_conv_b, ml_i_b, ml_f_b, ml_norm_g,
           attn_sinks, attn_norm_g, w_out, post_mix_g, pre_ffn_g, w_up, ffn_conv_w, ffn_conv_b,
           w_down, post_ffn_g):
    bsz, seq, d = x.shape
    depth = w_ada.shape[0]
    ts = min(1024, seq)
    cos, sin = _rope_tables(seq)
    for l in range(depth):
        mod = _adaln(c, w_ada[l], b_ada[l]).reshape(bsz, N_MOD, d)
        win = _relayout_in_proj(w_in[l])
        gb = jnp.concatenate([ml_i_b[l], ml_f_b[l], jnp.zeros((LANES - 2 * ML_H,), F32)]).reshape(1, LANES)
        sinks = jnp.broadcast_to(attn_sinks[l][:, None], (ATT_H, LANES))
        x = _mixer(x, mod, pre_mix_g[l].reshape(1, d), win, ml_conv_w[l], ml_conv_b[l].reshape(1, -1),
                   gb, ml_norm_g[l].reshape(1, ML_W), sinks, attn_norm_g[l].reshape(1, ATT_W),
                   cos, sin, w_out[l].astype(BF16), post_mix_g[l].reshape(1, d), ts)
        half_value = jnp.concatenate([jnp.ones((D_FF,), F32), jnp.full((D_FF,), 0.5, F32)])
        x = _ffn(x, mod, pre_ffn_g[l].reshape(1, d), w_up[l].astype(BF16),
                 ffn_conv_w[l] * half_value, (ffn_conv_b[l] * half_value).reshape(1, -1),
                 w_down[l].astype(BF16), post_ffn_g[l].reshape(1, d), ts)
    return x
```

```python
import functools
import math

import numpy as np
import jax
import jax.numpy as jnp
from jax import lax
from jax.experimental import pallas as pl
from jax.experimental.pallas import tpu as pltpu

F32 = jnp.float32
BF16 = jnp.bfloat16

D_MODEL = 1024
ML_H = 4
ML_HD = 128
ML_W = ML_H * ML_HD
ML_CONV = 4
ATT_H = 8
ATT_KV = 2
ATT_HD = 64
ATT_W = ATT_H * ATT_HD
WINDOW = 128
ROPE_THETA = 10000.0
MIX_W = ML_W + ATT_W
IN_COLS = 4 * ML_W + 2 * ML_H + ATT_W + 2 * ATT_KV * ATT_HD
D_FF = 2816
FFN_CONV = 3
N_MOD = 6
RMS_EPS = 1e-6

LANES = 128
SUBLANES = 8
BLK = 128
NEG = -1e30
LOG2E = math.log2(math.e)

C_QK = 0
C_V = 2 * ML_W
C_O = 3 * ML_W
C_G = 4 * ML_W
C_QA = C_G + LANES
C_KA = C_QA + ATT_W
C_VA = C_KA + ATT_KV * LANES
NW = C_VA + ATT_KV * LANES

FF_TILE = 256
N_FF_TILES = D_FF // FF_TILE

VMEM_LIMIT = 56 * 1024 * 1024

SLAB_PAD_ROWS = 8
MLSTM_BLOCKS_PER_STAGE = 2
ATT_BLOCKS_PER_STAGE = 1


def _relayout_in_proj(w):
    d = w.shape[0]
    half = ATT_HD // 2
    g0 = 4 * ML_W
    q0 = g0 + 2 * ML_H
    k0 = q0 + ATT_W
    v0 = k0 + ATT_KV * ATT_HD
    gates = jnp.pad(w[:, g0:q0], ((0, 0), (0, LANES - 2 * ML_H)))
    wq = w[:, q0:k0].reshape(d, ATT_H // 2, 2, 2, half).transpose(0, 1, 3, 2, 4).reshape(d, ATT_W)
    wk = jnp.broadcast_to(w[:, k0:v0].reshape(d, ATT_KV, 2, 1, half),
                          (d, ATT_KV, 2, 2, half)).reshape(d, ATT_KV * LANES)
    wv = jnp.broadcast_to(w[:, v0:].reshape(d, ATT_KV, 1, ATT_HD),
                          (d, ATT_KV, 2, ATT_HD)).reshape(d, ATT_KV * LANES)
    return jnp.concatenate([w[:, :g0], gates, wq, wk, wv], axis=1).astype(BF16)


def _sigmoid(v):
    return 1.0 / (1.0 + jnp.exp(-v))


def _dot(a, b):
    return jnp.dot(a, b, preferred_element_type=F32)


def _dot_nt(a, b):
    return lax.dot_general(a, b, (((1,), (1,)), ((), ())), preferred_element_type=F32)


def _dot_tn(a, b):
    return lax.dot_general(a, b, (((0,), (0,)), ((), ())), preferred_element_type=F32)


def _rms(v):
    return v * lax.rsqrt(jnp.mean(v * v, axis=-1, keepdims=True) + RMS_EPS)


def _adaln_kernel(c_ref, w_ref, b_ref, o_ref):
    c = c_ref[...]
    ca = (c * _sigmoid(c)).astype(BF16)
    o_ref[...] = _dot(ca, w_ref[...].astype(BF16)) + b_ref[...]


def _adaln(c, w, b):
    bsz, d = c.shape
    n = w.shape[1]
    tn = 1024
    return pl.pallas_call(
        _adaln_kernel,
        grid=(n // tn,),
        in_specs=[
            pl.BlockSpec((bsz, d), lambda j: (0, 0)),
            pl.BlockSpec((d, tn), lambda j: (0, j)),
            pl.BlockSpec((1, tn), lambda j: (0, j)),
        ],
        out_specs=pl.BlockSpec((bsz, tn), lambda j: (0, j)),
        out_shape=jax.ShapeDtypeStruct((bsz, n), F32),
        compiler_params=pltpu.CompilerParams(dimension_semantics=("arbitrary",)),
        name="adaln",
    )(c, w, b.reshape(1, n))


def _dup(v):
    return jnp.concatenate([v, v], axis=1)


def _mlstm_tile(q_s, kt_ml_s, v_s, og_s, bc_s, ipt_s, bt_s, mlg_ref, mix_s, st_s, m_s, nblk, tril, ones_b):
    for r0 in range(0, nblk, MLSTM_BLOCKS_PER_STAGE):
        units = [(r, j) for r in range(r0, min(r0 + MLSTM_BLOCKS_PER_STAGE, nblk)) for j in range(ML_H)]
        rows = {u: slice(u[0] * BLK, (u[0] + 1) * BLK) for u in units}
        hcol = {u: slice(u[1] * ML_HD, (u[1] + 1) * ML_HD) for u in units}
        a_row, b_last, g_row, gmax = {}, {}, {}, {}
        for u in units:
            r, j = u
            i_row = ipt_s[r, j:j + 1, :]
            b_row = bt_s[r, ML_H + j:ML_H + j + 1, :]
            a_row[u] = i_row - b_row
            b_last[u] = b_row[:, BLK - 1:BLK]
            g_row[u] = b_last[u] - b_row + i_row
            gmax[u] = jnp.max(g_row[u], axis=-1, keepdims=True)
        dma = {u: jnp.where(tril, a_row[u], NEG) for u in units}
        cm_col = {u: jnp.max(dma[u], axis=-1, keepdims=True) for u in units}
        cm = {u: jnp.broadcast_to(cm_col[u], (BLK, LANES)) for u in units}
        pw = {u: jnp.exp(dma[u] - cm[u]) for u in units}
        qk = {u: _dot(q_s[rows[u], hcol[u]], kt_ml_s[u[0], u[1]]) for u in units}
        nloc, aloc = {}, {}
        for u in units:
            v_aug = jnp.concatenate([v_s[rows[u], hcol[u]], ones_b], axis=1)
            nloc[u] = _dot((qk[u] * pw[u]).astype(BF16), v_aug)
        for u in units:
            v_aug = jnp.concatenate([v_s[rows[u], hcol[u]], ones_b], axis=1)
            kw = (kt_ml_s[u[0], u[1]].astype(F32) * jnp.exp(g_row[u] - gmax[u])).astype(BF16)
            aloc[u] = _dot(kw, v_aug)
        m_prev, st_b = {}, {}
        for j in range(ML_H):
            st = st_s[j]
            m = m_s[j:j + 1, :]
            for r in range(r0, min(r0 + MLSTM_BLOCKS_PER_STAGE, nblk)):
                u = (r, j)
                m_prev[u] = m
                st_b[u] = st.astype(BF16)
                m_new = jnp.maximum(b_last[u] + m, gmax[u])
                decay = jnp.exp(b_last[u] + m - m_new)
                gain = jnp.exp(gmax[u] - m_new)
                st = _dup(decay) * st + _dup(gain) * aloc[u]
                m = m_new
            st_s[j] = st
            m_s[j:j + 1, :] = m
        inter = {u: _dot(q_s[rows[u], hcol[u]], st_b[u]) for u in units}
        hv, mm = {}, {}
        for u in units:
            mm[u] = jnp.maximum(m_prev[u], cm[u])
            num = nloc[u] * _dup(jnp.exp(cm[u] - mm[u])) + _dup(jnp.exp(m_prev[u] - mm[u])) * inter[u]
            b_rep = jnp.broadcast_to(bc_s[u[0], :, ML_H + u[1]:ML_H + u[1] + 1], (BLK, LANES))
            floor = jnp.exp(-(b_rep + mm[u]))
            hv[u] = num[:, 0:ML_HD] / jnp.maximum(jnp.abs(num[:, ML_HD:2 * ML_HD]), floor)
        ms = {u: _dot((hv[u] * hv[u]).astype(BF16), ones_b) * (1.0 / ML_HD) for u in units}
        for u in units:
            hn = hv[u] * lax.rsqrt(ms[u] + RMS_EPS) * mlg_ref[:, hcol[u]]
            mix_s[rows[u], hcol[u]] = (hn * _sigmoid(og_s[rows[u], hcol[u]])).astype(BF16)


def _attention_tile(qa_s, kt_s, vt_s, sink_ref, atg_ref, mix_s, nblk, has_prev_tile, lane, ones_b):
    head_a = (lane % (LANES // 2)) < (LANES // 4)
    lane_lo = lane < (LANES // 2)
    rows4 = lax.broadcasted_iota(jnp.int32, (4 * BLK, BLK), 0) % BLK
    cols4 = lax.broadcasted_iota(jnp.int32, (4 * BLK, BLK), 1)
    from_prev = cols4 > rows4
    prev_bias = jnp.where(has_prev_tile, 0.0, NEG)
    sink4 = {}
    for g in range(ATT_KV):
        sink4[g] = jnp.concatenate(
            [jnp.broadcast_to(sink_ref[h:h + 1, :], (BLK, LANES)) for h in range(4 * g, 4 * g + 4)], axis=0)
    for r0 in range(0, nblk, ATT_BLOCKS_PER_STAGE):
        blocks = range(r0, min(r0 + ATT_BLOCKS_PER_STAGE, nblk))
        units = [(r, g) for r in blocks for g in range(ATT_KV)]
        sc = {}
        for u in units:
            r, g = u
            rows = slice(r * BLK, (r + 1) * BLK)
            qs = []
            for p in (2 * g, 2 * g + 1):
                qp = qa_s[rows, p * LANES:(p + 1) * LANES]
                qs.append(jnp.where(head_a, qp, jnp.zeros_like(qp)))
                qs.append(jnp.where(head_a, jnp.zeros_like(qp), qp))
            raw = _dot_nt(jnp.concatenate(qs, axis=0), kt_s[g, r * BLK:(r + 2) * BLK, :])
            raw_prev = raw[:, 0:BLK] + prev_bias if r == 0 else raw[:, 0:BLK]
            sc[u] = jnp.where(from_prev, raw_prev, raw[:, BLK:2 * BLK])
        m_col = {u: jnp.max(sc[u], axis=-1, keepdims=True) for u in units}
        m = {u: jnp.maximum(jnp.broadcast_to(m_col[u], (4 * BLK, LANES)), sink4[u[1]]) for u in units}
        pe = {u: jnp.exp2(sc[u] - m[u]).astype(BF16) for u in units}
        o4a = {}
        for u in units:
            zero = jnp.zeros_like(pe[u])
            banded = jnp.concatenate([jnp.where(from_prev, pe[u], zero), jnp.where(from_prev, zero, pe[u])], axis=1)
            o4a[u] = _dot(banded, vt_s[u[1], u[0] * BLK:(u[0] + 2) * BLK, :])
        for r in blocks:
            rows = slice(r * BLK, (r + 1) * BLK)
            pair_out = []
            for g in range(ATT_KV):
                u = (r, g)
                o4 = o4a[u][:, 0:LANES] / (o4a[u][:, LANES:2 * LANES] + jnp.exp2(sink4[g] - m[u]))
                for pp in range(2):
                    pair_out.append(jnp.where(lane_lo, o4[(2 * pp) * BLK:(2 * pp + 1) * BLK, :],
                                              o4[(2 * pp + 1) * BLK:(2 * pp + 2) * BLK, :]))
            sq = sum(po * po for po in pair_out)
            rs = lax.rsqrt(_dot(sq.astype(BF16), ones_b) * (1.0 / ATT_W) + RMS_EPS)
            for p in range(ATT_H // 2):
                mix_s[rows, ML_W + p * LANES:ML_W + (p + 1) * LANES] = (
                    pair_out[p] * rs * atg_ref[:, p * LANES:(p + 1) * LANES]).astype(BF16)


def _mixer_kernel(x_ref, mod_ref, gpre_ref, win_ref, cw_ref, cb_ref, gb_ref, mlg_ref, sink_ref,
                  atg_ref, cos_ref, sin_ref, wout_ref, gpost_ref, o_ref,
                  h_s, xp_s, q_s, kt_ml_s, v_s, og_s, bc_s, ipt_s, bt_s, qa_s, kt_s, vt_s,
                  mix_s, st_s, m_s):
    s_idx = pl.program_id(1)
    ts = x_ref.shape[0]
    nblk = ts // BLK

    @pl.when(s_idx == 0)
    def _reset_state():
        xp_s[0:SUBLANES, :] = jnp.zeros((SUBLANES, 2 * ML_W), F32)
        kt_s[:, 0:BLK, :] = jnp.zeros((ATT_KV, BLK, LANES), BF16)
        vt_s[:, :, 0:LANES] = jnp.zeros((ATT_KV, BLK + ts, LANES), BF16)
        vt_s[:, :, LANES:2 * LANES] = jnp.ones((ATT_KV, BLK + ts, LANES), BF16)
        st_s[...] = jnp.zeros(st_s.shape, F32)
        m_s[...] = jnp.zeros(m_s.shape, F32)

    @pl.when(s_idx > 0)
    def _carry_kv():
        for g in range(ATT_KV):
            kt_s[g, 0:BLK, :] = kt_s[g, ts:ts + BLK, :]
            vt_s[g, 0:BLK, 0:LANES] = vt_s[g, ts:ts + BLK, 0:LANES]

    x = x_ref[...]
    h = _rms(x) * (gpre_ref[...] * (1.0 + mod_ref[1:2, :])) + mod_ref[0:1, :]
    h_s[...] = h.astype(BF16)

    kscale = 1.0 / math.sqrt(ML_HD)
    cchunk = 256
    for cj in range(2 * ML_W // cchunk):
        cols = slice(cj * cchunk, (cj + 1) * cchunk)
        u = _dot(h_s[...], win_ref[:, C_QK + cj * cchunk:C_QK + (cj + 1) * cchunk])
        xp_s[SUBLANES:SUBLANES + ts, cols] = u
        w = cw_ref[:, cols]
        y = (u * w[3:4] + xp_s[SUBLANES - 1:SUBLANES - 1 + ts, cols] * w[2:3]
             + xp_s[SUBLANES - 2:SUBLANES - 2 + ts, cols] * w[1:2]
             + xp_s[SUBLANES - 3:SUBLANES - 3 + ts, cols] * w[0:1] + cb_ref[:, cols])
        a = y * _sigmoid(y)
        if cj * cchunk < ML_W:
            q_s[:, cols] = a.astype(BF16)
        else:
            ak = a * kscale
            for jj in range(cchunk // ML_HD):
                j = (cj * cchunk - ML_W) // ML_HD + jj
                for r in range(nblk):
                    kt_ml_s[r, j] = ak[r * BLK:(r + 1) * BLK, jj * ML_HD:(jj + 1) * ML_HD].T.astype(BF16)
        xp_s[0:SUBLANES, cols] = u[ts - SUBLANES:ts, :]

    v_s[...] = _dot(h_s[...], win_ref[:, C_V:C_V + ML_W]).astype(BF16)
    og_s[...] = _dot(h_s[...], win_ref[:, C_O:C_O + ML_W])

    gts = _dot(h_s[...], win_ref[:, C_G:C_G + LANES]) + gb_ref[...]
    lsg = jnp.minimum(gts, 0.0) - jnp.log(1.0 + jnp.exp(-jnp.abs(gts)))
    row_i = lax.broadcasted_iota(jnp.int32, (BLK, BLK), 0)
    col_i = lax.broadcasted_iota(jnp.int32, (BLK, BLK), 1)
    tril = row_i >= col_i
    tril_b = jnp.where(tril, 1.0, 0.0).astype(BF16)
    for r in range(nblk):
        ipr = gts[r * BLK:(r + 1) * BLK, :]
        lsr = lsg[r * BLK:(r + 1) * BLK, :]
        hi = lsr.astype(BF16)
        r1 = lsr - hi.astype(F32)
        mid = r1.astype(BF16)
        lo = (r1 - mid.astype(F32)).astype(BF16)
        bc = _dot(tril_b, hi) + _dot(tril_b, mid) + _dot(tril_b, lo)
        bc_s[r] = bc
        ipt_s[r] = ipr.T[0:SUBLANES, :]
        bt_s[r] = bc.T[0:SUBLANES, :]

    cos = cos_ref[...]
    sin = sin_ref[...]
    qat = _dot(h_s[...], win_ref[:, C_QA:C_QA + ATT_W])
    qscale = LOG2E / math.sqrt(ATT_HD)
    for p in range(ATT_H // 2):
        qp = qat[:, p * LANES:(p + 1) * LANES]
        qr = (qp * cos + pltpu.roll(qp, LANES // 2, 1) * sin) * qscale
        qa_s[:, p * LANES:(p + 1) * LANES] = qr.astype(BF16)
    kat = _dot(h_s[...], win_ref[:, C_KA:C_KA + ATT_KV * LANES])
    vat = _dot(h_s[...], win_ref[:, C_VA:C_VA + ATT_KV * LANES])
    for g in range(ATT_KV):
        kp = kat[:, g * LANES:(g + 1) * LANES]
        kr = kp * cos + pltpu.roll(kp, LANES // 2, 1) * sin
        kt_s[g, BLK:BLK + ts, :] = kr.astype(BF16)
        vt_s[g, BLK:BLK + ts, 0:LANES] = vat[:, g * LANES:(g + 1) * LANES].astype(BF16)

    lane = lax.broadcasted_iota(jnp.int32, (BLK, LANES), 1)
    ones_b = jnp.ones((BLK, LANES), BF16)
    _mlstm_tile(q_s, kt_ml_s, v_s, og_s, bc_s, ipt_s, bt_s, mlg_ref, mix_s, st_s, m_s, nblk, tril, ones_b)
    _attention_tile(qa_s, kt_s, vt_s, sink_ref, atg_ref, mix_s, nblk, s_idx > 0, lane, ones_b)

    y = _dot(mix_s[...], wout_ref[...])
    o_ref[...] = x_ref[...] + _rms(y) * (mod_ref[2:3, :] * gpost_ref[...])


def _const_spec(shape):
    nd = len(shape)
    return pl.BlockSpec(shape, lambda b, s: (0,) * nd, pipeline_mode=pl.Buffered(1))


def _mixer(x, mod, gpre, win, cw, cb, gb, mlg, sinks, atg, cos, sin, wout, gpost, ts):
    bsz, seq, d = x.shape
    nblk = ts // BLK
    in_specs = [
        pl.BlockSpec((None, ts, d), lambda b, s: (b, s, 0)),
        pl.BlockSpec((None, N_MOD, d), lambda b, s: (b, 0, 0)),
        _const_spec((1, d)),
        _const_spec((d, NW)),
        _const_spec((ML_CONV, 2 * ML_W)),
        _const_spec((1, 2 * ML_W)),
        _const_spec((1, LANES)),
        _const_spec((1, ML_W)),
        _const_spec((ATT_H, LANES)),
        _const_spec((1, ATT_W)),
        pl.BlockSpec((ts, LANES), lambda b, s: (s, 0)),
        pl.BlockSpec((ts, LANES), lambda b, s: (s, 0)),
        _const_spec((MIX_W, d)),
        _const_spec((1, d)),
    ]
    scratch = [
        pltpu.VMEM((ts, d), BF16),
        pltpu.VMEM((SUBLANES + ts, 2 * ML_W), F32),
        pltpu.VMEM((ts, ML_W), BF16),
        pltpu.VMEM((nblk, ML_H, ML_HD, BLK), BF16),
        pltpu.VMEM((ts, ML_W), BF16),
        pltpu.VMEM((ts, ML_W), F32),
        pltpu.VMEM((nblk, BLK, LANES), F32),
        pltpu.VMEM((nblk, SUBLANES, BLK), F32),
        pltpu.VMEM((nblk, SUBLANES, BLK), F32),
        pltpu.VMEM((ts, ATT_W), BF16),
        pltpu.VMEM((ATT_KV, BLK + ts, LANES), BF16),
        pltpu.VMEM((ATT_KV, BLK + ts, 2 * LANES), BF16),
        pltpu.VMEM((ts, MIX_W), BF16),
        pltpu.VMEM((ML_H, ML_HD, 2 * ML_HD), F32),
        pltpu.VMEM((SUBLANES, LANES), F32),
    ]
    return pl.pallas_call(
        _mixer_kernel,
        grid=(bsz, seq // ts),
        in_specs=in_specs,
        out_specs=pl.BlockSpec((None, ts, d), lambda b, s: (b, s, 0)),
        out_shape=jax.ShapeDtypeStruct((bsz, seq, d), F32),
        scratch_shapes=scratch,
        compiler_params=pltpu.CompilerParams(
            dimension_semantics=("arbitrary", "arbitrary"), vmem_limit_bytes=VMEM_LIMIT),
        name="mixer",
    )(x, mod, gpre, win, cw, cb, gb, mlg, sinks, atg, cos, sin, wout, gpost)


def _ffn_kernel(x_ref, mod_ref, gpre_ref, wup_ref, cw_ref, cb_ref, wdn_ref, gpost_ref, o_ref,
                slab_s, h_s, ub_s, carry_s, act_s):
    s_idx = pl.program_id(1)
    ts = x_ref.shape[0]
    nv = ts // SUBLANES
    pitch = slab_s.shape[1] // SUBLANES
    nslab = slab_s.shape[0]

    @pl.when(s_idx == 0)
    def _reset_state():
        carry_s[...] = jnp.zeros(carry_s.shape, F32)

    x = x_ref[...]
    h = _rms(x) * (gpre_ref[...] * (1.0 + mod_ref[4:5, :])) + mod_ref[3:4, :]
    for c in range(nslab):
        for s in range(SUBLANES):
            slab_s[c, s * pitch:s * pitch + nv, :] = h[s * nv:(s + 1) * nv, c * LANES:(c + 1) * LANES]
    for c in range(nslab):
        col = jnp.concatenate([slab_s[c, pl.ds(v, SUBLANES, stride=pitch), :] for v in range(nv)], axis=0)
        h_s[:, c * LANES:(c + 1) * LANES] = col.astype(BF16)

    def up_proj(j):
        return tuple(_dot(h_s[...], wup_ref[:, half * D_FF + j * FF_TILE:half * D_FF + (j + 1) * FF_TILE])
                     for half in range(2))

    first_sublane = lax.broadcasted_iota(jnp.int32, (SUBLANES, FF_TILE), 0) == 0

    def causal_conv(u, j, half):
        cols = slice(half * D_FF + j * FF_TILE, half * D_FF + (j + 1) * FF_TILE)
        slot = 2 * (j % 2) + half
        prev = carry_s[j, half]
        for k in range(2):
            cur = u[ts - (2 - k) * SUBLANES:ts - (1 - k) * SUBLANES, :]
            ub_s[slot, k * SUBLANES:(k + 1) * SUBLANES, :] = jnp.where(
                first_sublane, pltpu.roll(prev[k * SUBLANES:(k + 1) * SUBLANES, :], 1, 0), pltpu.roll(cur, 1, 0))
        ub_s[slot, 2 * SUBLANES:2 * SUBLANES + ts, :] = u
        carry_s[j, half] = u[ts - 2 * SUBLANES:ts, :]
        w = cw_ref[:, cols]
        return (u * w[2:3] + ub_s[slot, SUBLANES:SUBLANES + ts, :] * w[1:2]
                + ub_s[slot, 0:ts, :] * w[0:1] + cb_ref[:, cols])

    c1 = math.sqrt(2.0 / math.pi)
    c3 = c1 * 0.044715
    for j in range(N_FF_TILES):
        ug, uv = up_proj(j)
        g = causal_conv(ug, j, 0)
        half_val = causal_conv(uv, j, 1)
        th = jnp.tanh(g * (c3 * (g * g) + c1))
        gv = g * half_val
        act_s[:, j * FF_TILE:(j + 1) * FF_TILE] = (gv * th + gv).astype(BF16)

    y2 = _dot(act_s[...], wdn_ref[...])
    delta = _rms(y2) * (mod_ref[5:6, :] * gpost_ref[...])

    for c in range(nslab):
        for v in range(nv):
            slab_s[c, pl.ds(v, SUBLANES, stride=pitch), :] = delta[v * SUBLANES:(v + 1) * SUBLANES,
                                                                   c * LANES:(c + 1) * LANES]
    for c in range(nslab):
        for s in range(SUBLANES):
            rows = slice(s * nv, (s + 1) * nv)
            lanes = slice(c * LANES, (c + 1) * LANES)
            o_ref[rows, lanes] = x_ref[rows, lanes] + slab_s[c, s * pitch:s * pitch + nv, :]


def _ffn(x, mod, gpre, wup, cw, cb, wdn, gpost, ts):
    bsz, seq, d = x.shape
    in_specs = [
        pl.BlockSpec((None, ts, d), lambda b, s: (b, s, 0)),
        pl.BlockSpec((None, N_MOD, d), lambda b, s: (b, 0, 0)),
        _const_spec((1, d)),
        _const_spec((d, 2 * D_FF)),
        _const_spec((FFN_CONV, 2 * D_FF)),
        _const_spec((1, 2 * D_FF)),
        _const_spec((D_FF, d)),
        _const_spec((1, d)),
    ]
    scratch = [
        pltpu.VMEM((d // LANES, ts + SUBLANES * SLAB_PAD_ROWS, LANES), F32),
        pltpu.VMEM((ts, d), BF16),
        pltpu.VMEM((4, 2 * SUBLANES + ts, FF_TILE), F32),
        pltpu.VMEM((N_FF_TILES, 2, 2 * SUBLANES, FF_TILE), F32),
        pltpu.VMEM((ts, D_FF), BF16),
    ]
    return pl.pallas_call(
        _ffn_kernel,
        grid=(bsz, seq // ts),
        in_specs=in_specs,
        out_specs=pl.BlockSpec((None, ts, d), lambda b, s: (b, s, 0)),
        out_shape=jax.ShapeDtypeStruct((bsz, seq, d), F32),
        scratch_shapes=scratch,
        compiler_params=pltpu.CompilerParams(
            dimension_semantics=("arbitrary", "arbitrary"), vmem_limit_bytes=VMEM_LIMIT),
        name="conv_ffn",
    )(x, mod, gpre, wup, cw, cb, wdn, gpost)


def _rope_tables(seq):
    half = ATT_HD // 2
    inv = ROPE_THETA ** (-jnp.arange(half, dtype=F32) / half)
    ang = jnp.arange(seq).astype(F32)[:, None] * inv[None, :]
    cos = jnp.cos(ang)
    sin = jnp.sin(ang)
    return jnp.tile(cos, (1, 4)), jnp.concatenate([-sin, -sin, sin, sin], axis=1)


def kernel(x, c, w_ada, b_ada, pre_mix_g, w_in, ml_conv_w, ml_conv_b, ml_i_b, ml_f_b, ml_norm_g,
           attn_sinks, attn_norm_g, w_out, post_mix_g, pre_ffn_g, w_up, ffn_conv_w, ffn_conv_b,
           w_down, post_ffn_g):
    bsz, seq, d = x.shape
    depth = w_ada.shape[0]
    ts = min(1024, seq)
    cos, sin = _rope_tables(seq)
    for l in range(depth):
        mod = _adaln(c, w_ada[l], b_ada[l]).reshape(bsz, N_MOD, d)
        win = _relayout_in_proj(w_in[l])
        gb = jnp.concatenate([ml_i_b[l], ml_f_b[l], jnp.zeros((LANES - 2 * ML_H,), F32)]).reshape(1, LANES)
        sinks = jnp.broadcast_to((attn_sinks[l] * LOG2E)[:, None], (ATT_H, LANES))
        x = _mixer(x, mod, pre_mix_g[l].reshape(1, d), win, ml_conv_w[l], ml_conv_b[l].reshape(1, -1),
                   gb, ml_norm_g[l].reshape(1, ML_W), sinks, attn_norm_g[l].reshape(1, ATT_W),
                   cos, sin, w_out[l].astype(BF16), post_mix_g[l].reshape(1, d), ts)
        half_value = jnp.concatenate([jnp.ones((D_FF,), F32), jnp.full((D_FF,), 0.5, F32)])
        x = _ffn(x, mod, pre_ffn_g[l].reshape(1, d), w_up[l].astype(BF16),
                 ffn_conv_w[l] * half_value, (ffn_conv_b[l] * half_value).reshape(1, -1),
                 w_down[l].astype(BF16), post_ffn_g[l].reshape(1, d), ts)
    return x
```

```python
import functools
import math

import numpy as np
import jax
import jax.numpy as jnp
from jax import lax
from jax.experimental import pallas as pl
from jax.experimental.pallas import tpu as pltpu

F32 = jnp.float32
BF16 = jnp.bfloat16

D_MODEL = 1024
ML_H = 4
ML_HD = 128
ML_W = ML_H * ML_HD
ML_CONV = 4
ATT_H = 8
ATT_KV = 2
ATT_HD = 64
ATT_W = ATT_H * ATT_HD
WINDOW = 128
ROPE_THETA = 10000.0
MIX_W = ML_W + ATT_W
IN_COLS = 4 * ML_W + 2 * ML_H + ATT_W + 2 * ATT_KV * ATT_HD
D_FF = 2816
FFN_CONV = 3
N_MOD = 6
RMS_EPS = 1e-6

LANES = 128
SUBLANES = 8
BLK = 128
NEG = -1e30
LOG2E = math.log2(math.e)

C_QK = 0
C_V = 2 * ML_W
C_O = 3 * ML_W
C_G = 4 * ML_W
C_QA = C_G + LANES
C_KA = C_QA + ATT_W
C_VA = C_KA + ATT_KV * LANES
NW = C_VA + ATT_KV * LANES

FF_TILE = 256
N_FF_TILES = D_FF // FF_TILE

VMEM_LIMIT = 56 * 1024 * 1024

SLAB_PAD_ROWS = 8
MLSTM_BLOCKS_PER_STAGE = 2
ATT_BLOCKS_PER_STAGE = 4


def _relayout_in_proj(w):
    d = w.shape[0]
    half = ATT_HD // 2
    g0 = 4 * ML_W
    q0 = g0 + 2 * ML_H
    k0 = q0 + ATT_W
    v0 = k0 + ATT_KV * ATT_HD
    gates = jnp.pad(w[:, g0:q0], ((0, 0), (0, LANES - 2 * ML_H)))
    wq = w[:, q0:k0].reshape(d, ATT_H // 2, 2, 2, half).transpose(0, 1, 3, 2, 4).reshape(d, ATT_W)
    wk = jnp.broadcast_to(w[:, k0:v0].reshape(d, ATT_KV, 2, 1, half),
                          (d, ATT_KV, 2, 2, half)).reshape(d, ATT_KV * LANES)
    wv = jnp.broadcast_to(w[:, v0:].reshape(d, ATT_KV, 1, ATT_HD),
                          (d, ATT_KV, 2, ATT_HD)).reshape(d, ATT_KV * LANES)
    return jnp.concatenate([w[:, :g0], gates, wq, wk, wv], axis=1).astype(BF16)


def _sigmoid(v):
    return 1.0 / (1.0 + jnp.exp(-v))


def _dot(a, b):
    return jnp.dot(a, b, preferred_element_type=F32)


def _dot_nt(a, b):
    return lax.dot_general(a, b, (((1,), (1,)), ((), ())), preferred_element_type=F32)


def _dot_tn(a, b):
    return lax.dot_general(a, b, (((0,), (0,)), ((), ())), preferred_element_type=F32)


def _mod_row(mod_ref, k):
    return mod_ref[pl.ds(pl.program_id(0), 1), k * D_MODEL:(k + 1) * D_MODEL]


def _rms(v):
    return v * lax.rsqrt(jnp.mean(v * v, axis=-1, keepdims=True) + RMS_EPS)


def _adaln_kernel(c_ref, w_ref, b_ref, o_ref):
    c = c_ref[...]
    ca = (c * _sigmoid(c)).astype(BF16)
    o_ref[...] = _dot(ca, w_ref[...].astype(BF16)) + b_ref[...]


def _adaln(c, w, b):
    bsz, d = c.shape
    n = w.shape[1]
    tn = 1024
    return pl.pallas_call(
        _adaln_kernel,
        grid=(n // tn,),
        in_specs=[
            pl.BlockSpec((bsz, d), lambda j: (0, 0)),
            pl.BlockSpec((d, tn), lambda j: (0, j)),
            pl.BlockSpec((1, tn), lambda j: (0, j)),
        ],
        out_specs=pl.BlockSpec((bsz, tn), lambda j: (0, j)),
        out_shape=jax.ShapeDtypeStruct((bsz, n), F32),
        compiler_params=pltpu.CompilerParams(dimension_semantics=("arbitrary",)),
        name="adaln",
    )(c, w, b.reshape(1, n))


def _dup(v):
    return jnp.concatenate([v, v], axis=1)


def _mlstm_tile(q_s, kt_ml_s, v_s, og_s, bc_s, ipt_s, bt_s, mlg_ref, mix_s, st_s, m_s, nblk, tril, ones_b):
    for r0 in range(0, nblk, MLSTM_BLOCKS_PER_STAGE):
        units = [(r, j) for r in range(r0, min(r0 + MLSTM_BLOCKS_PER_STAGE, nblk)) for j in range(ML_H)]
        rows = {u: slice(u[0] * BLK, (u[0] + 1) * BLK) for u in units}
        hcol = {u: slice(u[1] * ML_HD, (u[1] + 1) * ML_HD) for u in units}
        a_row, b_last, g_row, gmax = {}, {}, {}, {}
        for u in units:
            r, j = u
            i_row = ipt_s[r, j:j + 1, :]
            b_row = bt_s[r, ML_H + j:ML_H + j + 1, :]
            a_row[u] = i_row - b_row
            b_last[u] = b_row[:, BLK - 1:BLK]
            g_row[u] = b_last[u] - b_row + i_row
            gmax[u] = jnp.max(g_row[u], axis=-1, keepdims=True)
        dma = {u: jnp.where(tril, a_row[u], NEG) for u in units}
        cm_col = {u: jnp.max(dma[u], axis=-1, keepdims=True) for u in units}
        cm = {u: jnp.broadcast_to(cm_col[u], (BLK, LANES)) for u in units}
        pw = {u: jnp.exp2(dma[u] - cm[u]) for u in units}
        qk = {u: _dot(q_s[rows[u], hcol[u]], kt_ml_s[u[0], u[1]]) for u in units}
        nloc, aloc = {}, {}
        for u in units:
            v_aug = jnp.concatenate([v_s[rows[u], hcol[u]], ones_b], axis=1)
            nloc[u] = _dot((qk[u] * pw[u]).astype(BF16), v_aug)
        for u in units:
            v_aug = jnp.concatenate([v_s[rows[u], hcol[u]], ones_b], axis=1)
            kw = (kt_ml_s[u[0], u[1]].astype(F32) * jnp.exp2(g_row[u] - gmax[u])).astype(BF16)
            aloc[u] = _dot(kw, v_aug)
        m_prev, st_b = {}, {}
        for j in range(ML_H):
            st = st_s[j]
            m = m_s[j:j + 1, :]
            for r in range(r0, min(r0 + MLSTM_BLOCKS_PER_STAGE, nblk)):
                u = (r, j)
                m_prev[u] = m
                st_b[u] = st.astype(BF16)
                m_new = jnp.maximum(b_last[u] + m, gmax[u])
                decay = jnp.exp2(b_last[u] + m - m_new)
                gain = jnp.exp2(gmax[u] - m_new)
                st = _dup(decay) * st + _dup(gain) * aloc[u]
                m = m_new
            st_s[j] = st
            m_s[j:j + 1, :] = m
        inter = {u: _dot(q_s[rows[u], hcol[u]], st_b[u]) for u in units}
        hv, mm = {}, {}
        for u in units:
            mm[u] = jnp.maximum(m_prev[u], cm[u])
            num = nloc[u] * _dup(jnp.exp2(cm[u] - mm[u])) + _dup(jnp.exp2(m_prev[u] - mm[u])) * inter[u]
            b_rep = jnp.broadcast_to(bc_s[u[0], :, ML_H + u[1]:ML_H + u[1] + 1], (BLK, LANES))
            floor = jnp.exp2(-(b_rep + mm[u]))
            hv[u] = num[:, 0:ML_HD] / jnp.maximum(jnp.abs(num[:, ML_HD:2 * ML_HD]), floor)
        ms = {u: _dot((hv[u] * hv[u]).astype(BF16), ones_b) * (1.0 / ML_HD) for u in units}
        for u in units:
            hn = hv[u] * lax.rsqrt(ms[u] + RMS_EPS) * mlg_ref[:, hcol[u]]
            mix_s[rows[u], hcol[u]] = (hn * _sigmoid(og_s[rows[u], hcol[u]])).astype(BF16)


def _attention_tile(qa_s, kt_s, vt_s, sink_ref, atg_ref, mix_s, nblk, has_prev_tile, lane, ones_b):
    head_a = (lane % (LANES // 2)) < (LANES // 4)
    lane_lo = lane < (LANES // 2)
    rows4 = lax.broadcasted_iota(jnp.int32, (4 * BLK, BLK), 0) % BLK
    cols4 = lax.broadcasted_iota(jnp.int32, (4 * BLK, BLK), 1)
    from_prev = cols4 > rows4
    prev_bias = jnp.where(has_prev_tile, 0.0, NEG)
    sink4 = {}
    for g in range(ATT_KV):
        sink4[g] = jnp.concatenate(
            [jnp.broadcast_to(sink_ref[h:h + 1, :], (BLK, LANES)) for h in range(4 * g, 4 * g + 4)], axis=0)
    for r0 in range(0, nblk, ATT_BLOCKS_PER_STAGE):
        blocks = range(r0, min(r0 + ATT_BLOCKS_PER_STAGE, nblk))
        units = [(r, g) for r in blocks for g in range(ATT_KV)]
        sc = {}
        for u in units:
            r, g = u
            rows = slice(r * BLK, (r + 1) * BLK)
            qs = []
            for p in (2 * g, 2 * g + 1):
                qp = qa_s[rows, p * LANES:(p + 1) * LANES]
                qs.append(jnp.where(head_a, qp, jnp.zeros_like(qp)))
                qs.append(jnp.where(head_a, jnp.zeros_like(qp), qp))
            raw = _dot_nt(jnp.concatenate(qs, axis=0), kt_s[g, r * BLK:(r + 2) * BLK, :])
            raw_prev = raw[:, 0:BLK] + prev_bias if r == 0 else raw[:, 0:BLK]
            sc[u] = jnp.where(from_prev, raw_prev, raw[:, BLK:2 * BLK])
        m_col = {u: jnp.max(sc[u], axis=-1, keepdims=True) for u in units}
        m = {u: jnp.maximum(jnp.broadcast_to(m_col[u], (4 * BLK, LANES)), sink4[u[1]]) for u in units}
        pe = {u: jnp.exp2(sc[u] - m[u]).astype(BF16) for u in units}
        o4a = {}
        for u in units:
            zero = jnp.zeros_like(pe[u])
            banded = jnp.concatenate([jnp.where(from_prev, pe[u], zero), jnp.where(from_prev, zero, pe[u])], axis=1)
            o4a[u] = _dot(banded, vt_s[u[1], u[0] * BLK:(u[0] + 2) * BLK, :])
        for r in blocks:
            rows = slice(r * BLK, (r + 1) * BLK)
            pair_out = []
            for g in range(ATT_KV):
                u = (r, g)
                o4 = o4a[u][:, 0:LANES] / (o4a[u][:, LANES:2 * LANES] + jnp.exp2(sink4[g] - m[u]))
                for pp in range(2):
                    pair_out.append(jnp.where(lane_lo, o4[(2 * pp) * BLK:(2 * pp + 1) * BLK, :],
                                              o4[(2 * pp + 1) * BLK:(2 * pp + 2) * BLK, :]))
            sq = sum(po * po for po in pair_out)
            rs = lax.rsqrt(_dot(sq.astype(BF16), ones_b) * (1.0 / ATT_W) + RMS_EPS)
            for p in range(ATT_H // 2):
                mix_s[rows, ML_W + p * LANES:ML_W + (p + 1) * LANES] = (
                    pair_out[p] * rs * atg_ref[:, p * LANES:(p + 1) * LANES]).astype(BF16)


def _mixer_kernel(x_ref, mod_ref, gpre_ref, win_ref, cw_ref, cb_ref, gb_ref, mlg_ref, sink_ref,
                  atg_ref, cos_ref, sin_ref, wout_ref, gpost_ref, o_ref,
                  h_s, xp_s, q_s, kt_ml_s, v_s, og_s, bc_s, ipt_s, bt_s, qa_s, kt_s, vt_s,
                  mix_s, st_s, m_s):
    s_idx = pl.program_id(1)
    ts = x_ref.shape[0]
    nblk = ts // BLK

    @pl.when(s_idx == 0)
    def _reset_state():
        xp_s[0:SUBLANES, :] = jnp.zeros((SUBLANES, 2 * ML_W), F32)
        kt_s[:, 0:BLK, :] = jnp.zeros((ATT_KV, BLK, LANES), BF16)
        vt_s[:, :, 0:LANES] = jnp.zeros((ATT_KV, BLK + ts, LANES), BF16)
        vt_s[:, :, LANES:2 * LANES] = jnp.ones((ATT_KV, BLK + ts, LANES), BF16)
        st_s[...] = jnp.zeros(st_s.shape, F32)
        m_s[...] = jnp.zeros(m_s.shape, F32)

    @pl.when(s_idx > 0)
    def _carry_kv():
        for g in range(ATT_KV):
            kt_s[g, 0:BLK, :] = kt_s[g, ts:ts + BLK, :]
            vt_s[g, 0:BLK, 0:LANES] = vt_s[g, ts:ts + BLK, 0:LANES]

    x = x_ref[...]
    h = _rms(x) * (gpre_ref[...] * (1.0 + _mod_row(mod_ref, 1))) + _mod_row(mod_ref, 0)
    h_s[...] = h.astype(BF16)

    kscale = 1.0 / math.sqrt(ML_HD)
    cchunk = 256
    for cj in range(2 * ML_W // cchunk):
        cols = slice(cj * cchunk, (cj + 1) * cchunk)
        u = _dot(h_s[...], win_ref[:, C_QK + cj * cchunk:C_QK + (cj + 1) * cchunk])
        xp_s[SUBLANES:SUBLANES + ts, cols] = u
        w = cw_ref[:, cols]
        y = (u * w[3:4] + xp_s[SUBLANES - 1:SUBLANES - 1 + ts, cols] * w[2:3]
             + xp_s[SUBLANES - 2:SUBLANES - 2 + ts, cols] * w[1:2]
             + xp_s[SUBLANES - 3:SUBLANES - 3 + ts, cols] * w[0:1] + cb_ref[:, cols])
        a = y * _sigmoid(y)
        if cj * cchunk < ML_W:
            q_s[:, cols] = a.astype(BF16)
        else:
            ak = a * kscale
            for jj in range(cchunk // ML_HD):
                j = (cj * cchunk - ML_W) // ML_HD + jj
                for r in range(nblk):
                    kt_ml_s[r, j] = ak[r * BLK:(r + 1) * BLK, jj * ML_HD:(jj + 1) * ML_HD].T.astype(BF16)
        xp_s[0:SUBLANES, cols] = u[ts - SUBLANES:ts, :]

    v_s[...] = _dot(h_s[...], win_ref[:, C_V:C_V + ML_W]).astype(BF16)
    og_s[...] = _dot(h_s[...], win_ref[:, C_O:C_O + ML_W])

    gts = _dot(h_s[...], win_ref[:, C_G:C_G + LANES]) + gb_ref[...]
    lsg = jnp.minimum(gts, 0.0) - jnp.log(1.0 + jnp.exp(-jnp.abs(gts)))
    row_i = lax.broadcasted_iota(jnp.int32, (BLK, BLK), 0)
    col_i = lax.broadcasted_iota(jnp.int32, (BLK, BLK), 1)
    tril = row_i >= col_i
    tril_b = jnp.where(tril, 1.0, 0.0).astype(BF16)
    for r in range(nblk):
        ipr = gts[r * BLK:(r + 1) * BLK, :]
        lsr = lsg[r * BLK:(r + 1) * BLK, :]
        hi = lsr.astype(BF16)
        r1 = lsr - hi.astype(F32)
        mid = r1.astype(BF16)
        lo = (r1 - mid.astype(F32)).astype(BF16)
        bc = _dot(tril_b, hi) + _dot(tril_b, mid) + _dot(tril_b, lo)
        bc2 = bc * LOG2E
        bc_s[r] = bc2
        ipt_s[r] = (ipr * LOG2E).T[0:SUBLANES, :]
        bt_s[r] = bc2.T[0:SUBLANES, :]

    cos = cos_ref[...]
    sin = sin_ref[...]
    qat = _dot(h_s[...], win_ref[:, C_QA:C_QA + ATT_W])
    qscale = LOG2E / math.sqrt(ATT_HD)
    for p in range(ATT_H // 2):
        qp = qat[:, p * LANES:(p + 1) * LANES]
        qr = (qp * cos + pltpu.roll(qp, LANES // 2, 1) * sin) * qscale
        qa_s[:, p * LANES:(p + 1) * LANES] = qr.astype(BF16)
    kat = _dot(h_s[...], win_ref[:, C_KA:C_KA + ATT_KV * LANES])
    vat = _dot(h_s[...], win_ref[:, C_VA:C_VA + ATT_KV * LANES])
    for g in range(ATT_KV):
        kp = kat[:, g * LANES:(g + 1) * LANES]
        kr = kp * cos + pltpu.roll(kp, LANES // 2, 1) * sin
        kt_s[g, BLK:BLK + ts, :] = kr.astype(BF16)
        vt_s[g, BLK:BLK + ts, 0:LANES] = vat[:, g * LANES:(g + 1) * LANES].astype(BF16)

    lane = lax.broadcasted_iota(jnp.int32, (BLK, LANES), 1)
    ones_b = jnp.ones((BLK, LANES), BF16)
    _mlstm_tile(q_s, kt_ml_s, v_s, og_s, bc_s, ipt_s, bt_s, mlg_ref, mix_s, st_s, m_s, nblk, tril, ones_b)
    _attention_tile(qa_s, kt_s, vt_s, sink_ref, atg_ref, mix_s, nblk, s_idx > 0, lane, ones_b)

    y = _dot(mix_s[...], wout_ref[...])
    o_ref[...] = x_ref[...] + _rms(y) * (_mod_row(mod_ref, 2) * gpost_ref[...])


def _const_spec(shape):
    nd = len(shape)
    return pl.BlockSpec(shape, lambda b, s: (0,) * nd, pipeline_mode=pl.Buffered(1))


def _mixer(x, mod, gpre, win, cw, cb, gb, mlg, sinks, atg, cos, sin, wout, gpost, ts):
    bsz, seq, d = x.shape
    nblk = ts // BLK
    in_specs = [
        pl.BlockSpec((None, ts, d), lambda b, s: (b, s, 0)),
        _const_spec((bsz, N_MOD * d)),
        _const_spec((1, d)),
        _const_spec((d, NW)),
        _const_spec((ML_CONV, 2 * ML_W)),
        _const_spec((1, 2 * ML_W)),
        _const_spec((1, LANES)),
        _const_spec((1, ML_W)),
        _const_spec((ATT_H, LANES)),
        _const_spec((1, ATT_W)),
        pl.BlockSpec((ts, LANES), lambda b, s: (s, 0)),
        pl.BlockSpec((ts, LANES), lambda b, s: (s, 0)),
        _const_spec((MIX_W, d)),
        _const_spec((1, d)),
    ]
    scratch = [
        pltpu.VMEM((ts, d), BF16),
        pltpu.VMEM((SUBLANES + ts, 2 * ML_W), F32),
        pltpu.VMEM((ts, ML_W), BF16),
        pltpu.VMEM((nblk, ML_H, ML_HD, BLK), BF16),
        pltpu.VMEM((ts, ML_W), BF16),
        pltpu.VMEM((ts, ML_W), F32),
        pltpu.VMEM((nblk, BLK, LANES), F32),
        pltpu.VMEM((nblk, SUBLANES, BLK), F32),
        pltpu.VMEM((nblk, SUBLANES, BLK), F32),
        pltpu.VMEM((ts, ATT_W), BF16),
        pltpu.VMEM((ATT_KV, BLK + ts, LANES), BF16),
        pltpu.VMEM((ATT_KV, BLK + ts, 2 * LANES), BF16),
        pltpu.VMEM((ts, MIX_W), BF16),
        pltpu.VMEM((ML_H, ML_HD, 2 * ML_HD), F32),
        pltpu.VMEM((SUBLANES, LANES), F32),
    ]
    return pl.pallas_call(
        _mixer_kernel,
        grid=(bsz, seq // ts),
        in_specs=in_specs,
        out_specs=pl.BlockSpec((None, ts, d), lambda b, s: (b, s, 0)),
        out_shape=jax.ShapeDtypeStruct((bsz, seq, d), F32),
        scratch_shapes=scratch,
        compiler_params=pltpu.CompilerParams(
            dimension_semantics=("arbitrary", "arbitrary"), vmem_limit_bytes=VMEM_LIMIT),
        name="mixer",
    )(x, mod, gpre, win, cw, cb, gb, mlg, sinks, atg, cos, sin, wout, gpost)


def _ffn_kernel(x_ref, mod_ref, gpre_ref, wup_ref, cw_ref, cb_ref, wdn_ref, gpost_ref, o_ref,
                slab_s, h_s, ub_s, carry_s, act_s):
    s_idx = pl.program_id(1)
    ts = x_ref.shape[0]
    nv = ts // SUBLANES
    pitch = slab_s.shape[1] // SUBLANES
    nslab = slab_s.shape[0]

    @pl.when(s_idx == 0)
    def _reset_state():
        carry_s[...] = jnp.zeros(carry_s.shape, F32)

    x = x_ref[...]
    h = _rms(x) * (gpre_ref[...] * (1.0 + _mod_row(mod_ref, 4))) + _mod_row(mod_ref, 3)
    for c in range(nslab):
        for s in range(SUBLANES):
            slab_s[c, s * pitch:s * pitch + nv, :] = h[s * nv:(s + 1) * nv, c * LANES:(c + 1) * LANES]
    for c in range(nslab):
        col = jnp.concatenate([slab_s[c, pl.ds(v, SUBLANES, stride=pitch), :] for v in range(nv)], axis=0)
        h_s[:, c * LANES:(c + 1) * LANES] = col.astype(BF16)

    def up_proj(j):
        return tuple(_dot(h_s[...], wup_ref[:, half * D_FF + j * FF_TILE:half * D_FF + (j + 1) * FF_TILE])
                     for half in range(2))

    first_sublane = lax.broadcasted_iota(jnp.int32, (SUBLANES, FF_TILE), 0) == 0

    def causal_conv(u, j, half):
        cols = slice(half * D_FF + j * FF_TILE, half * D_FF + (j + 1) * FF_TILE)
        slot = 2 * (j % 2) + half
        prev = carry_s[j, half]
        for k in range(2):
            cur = u[ts - (2 - k) * SUBLANES:ts - (1 - k) * SUBLANES, :]
            ub_s[slot, k * SUBLANES:(k + 1) * SUBLANES, :] = jnp.where(
                first_sublane, pltpu.roll(prev[k * SUBLANES:(k + 1) * SUBLANES, :], 1, 0), pltpu.roll(cur, 1, 0))
        ub_s[slot, 2 * SUBLANES:2 * SUBLANES + ts, :] = u
        carry_s[j, half] = u[ts - 2 * SUBLANES:ts, :]
        w = cw_ref[:, cols]
        return (u * w[2:3] + ub_s[slot, SUBLANES:SUBLANES + ts, :] * w[1:2]
                + ub_s[slot, 0:ts, :] * w[0:1] + cb_ref[:, cols])

    c1 = math.sqrt(2.0 / math.pi)
    c3 = c1 * 0.044715
    for j in range(N_FF_TILES):
        ug, uv = up_proj(j)
        g = causal_conv(ug, j, 0)
        half_val = causal_conv(uv, j, 1)
        th = jnp.tanh(g * (c3 * (g * g) + c1))
        gv = g * half_val
        act_s[:, j * FF_TILE:(j + 1) * FF_TILE] = (gv * th + gv).astype(BF16)

    y2 = _dot(act_s[...], wdn_ref[...])
    delta = _rms(y2) * (_mod_row(mod_ref, 5) * gpost_ref[...])

    for c in range(nslab):
        for v in range(nv):
            slab_s[c, pl.ds(v, SUBLANES, stride=pitch), :] = delta[v * SUBLANES:(v + 1) * SUBLANES,
                                                                   c * LANES:(c + 1) * LANES]
    for c in range(nslab):
        for s in range(SUBLANES):
            rows = slice(s * nv, (s + 1) * nv)
            lanes = slice(c * LANES, (c + 1) * LANES)
            o_ref[rows, lanes] = x_ref[rows, lanes] + slab_s[c, s * pitch:s * pitch + nv, :]


def _ffn(x, mod, gpre, wup, cw, cb, wdn, gpost, ts):
    bsz, seq, d = x.shape
    in_specs = [
        pl.BlockSpec((None, ts, d), lambda b, s: (b, s, 0)),
        _const_spec((bsz, N_MOD * d)),
        _const_spec((1, d)),
        _const_spec((d, 2 * D_FF)),
        _const_spec((FFN_CONV, 2 * D_FF)),
        _const_spec((1, 2 * D_FF)),
        _const_spec((D_FF, d)),
        _const_spec((1, d)),
    ]
    scratch = [
        pltpu.VMEM((d // LANES, ts + SUBLANES * SLAB_PAD_ROWS, LANES), F32),
        pltpu.VMEM((ts, d), BF16),
        pltpu.VMEM((4, 2 * SUBLANES + ts, FF_TILE), F32),
        pltpu.VMEM((N_FF_TILES, 2, 2 * SUBLANES, FF_TILE), F32),
        pltpu.VMEM((ts, D_FF), BF16),
    ]
    return pl.pallas_call(
        _ffn_kernel,
        grid=(bsz, seq // ts),
        in_specs=in_specs,
        out_specs=pl.BlockSpec((None, ts, d), lambda b, s: (b, s, 0)),
        out_shape=jax.ShapeDtypeStruct((bsz, seq, d), F32),
        scratch_shapes=scratch,
        compiler_params=pltpu.CompilerParams(
            dimension_semantics=("arbitrary", "arbitrary"), vmem_limit_bytes=VMEM_LIMIT),
        name="conv_ffn",
    )(x, mod, gpre, wup, cw, cb, wdn, gpost)


def _rope_tables(seq):
    half = ATT_HD // 2
    inv = ROPE_THETA ** (-jnp.arange(half, dtype=F32) / half)
    ang = jnp.arange(seq).astype(F32)[:, None] * inv[None, :]
    cos = jnp.cos(ang)
    sin = jnp.sin(ang)
    return jnp.tile(cos, (1, 4)), jnp.concatenate([-sin, -sin, sin, sin], axis=1)


def kernel(x, c, w_ada, b_ada, pre_mix_g, w_in, ml_conv_w, ml_conv_b, ml_i_b, ml_f_b, ml_norm_g,
           attn_sinks, attn_norm_g, w_out, post_mix_g, pre_ffn_g, w_up, ffn_conv_w, ffn_conv_b,
           w_down, post_ffn_g):
    bsz, seq, d = x.shape
    depth = w_ada.shape[0]
    ts = min(1024, seq)
    cos, sin = _rope_tables(seq)
    for l in range(depth):
        mod = _adaln(c, w_ada[l], b_ada[l])
        win = _relayout_in_proj(w_in[l])
        gb = jnp.concatenate([ml_i_b[l], ml_f_b[l], jnp.zeros((LANES - 2 * ML_H,), F32)]).reshape(1, LANES)
        sinks = jnp.broadcast_to((attn_sinks[l] * LOG2E)[:, None], (ATT_H, LANES))
        x = _mixer(x, mod, pre_mix_g[l].reshape(1, d), win, ml_conv_w[l], ml_conv_b[l].reshape(1, -1),
                   gb, ml_norm_g[l].reshape(1, ML_W), sinks, attn_norm_g[l].reshape(1, ATT_W),
                   cos, sin, w_out[l].astype(BF16), post_mix_g[l].reshape(1, d), ts)
        half_value = jnp.concatenate([jnp.ones((D_FF,), F32), jnp.full((D_FF,), 0.5, F32)])
        x = _ffn(x, mod, pre_ffn_g[l].reshape(1, d), w_up[l].astype(BF16),
                 ffn_conv_w[l] * half_value, (ffn_conv_b[l] * half_value).reshape(1, -1),
                 w_down[l].astype(BF16), post_ffn_g[l].reshape(1, d), ts)
    return x
```

```python
import math

import jax
import jax.numpy as jnp
from jax import lax
from jax.experimental import pallas as pl
from jax.experimental.pallas import tpu as pltpu

F32 = jnp.float32
BF16 = jnp.bfloat16

D_MODEL = 1024
ML_H = 4
ML_HD = 128
ML_W = ML_H * ML_HD
ML_CONV = 4
ATT_H = 8
ATT_KV = 2
ATT_HD = 64
ATT_W = ATT_H * ATT_HD
ROPE_THETA = 10000.0
MIX_W = ML_W + ATT_W
D_FF = 2816
FFN_CONV = 3
N_MOD = 6
RMS_EPS = 1e-6

LANES = 128
SUBLANES = 8
BLK = 128
NEG = -1e30
LOG2E = math.log2(math.e)

C_QK = 0
C_V = 2 * ML_W
C_O = 3 * ML_W
C_G = 4 * ML_W
C_QA = C_G + LANES
C_KA = C_QA + ATT_W
C_VA = C_KA + ATT_KV * LANES
NW = C_VA + ATT_KV * LANES

FF_TILE = 256
N_FF_TILES = D_FF // FF_TILE
QK_CONV_COLS = 256
ADALN_TILE = 1024
SEQ_TILE = 1024

VMEM_LIMIT = 56 * 1024 * 1024

SLAB_PAD_ROWS = 8
MLSTM_BLOCKS_PER_STAGE = 2
ATT_BLOCKS_PER_STAGE = 4


def _relayout_in_proj(w):
    d = w.shape[0]
    half = ATT_HD // 2
    g0 = 4 * ML_W
    q0 = g0 + 2 * ML_H
    k0 = q0 + ATT_W
    v0 = k0 + ATT_KV * ATT_HD
    gates = jnp.pad(w[:, g0:q0], ((0, 0), (0, LANES - 2 * ML_H)))
    wq = w[:, q0:k0].reshape(d, ATT_H // 2, 2, 2, half).transpose(0, 1, 3, 2, 4).reshape(d, ATT_W)
    wk = jnp.broadcast_to(w[:, k0:v0].reshape(d, ATT_KV, 2, 1, half),
                          (d, ATT_KV, 2, 2, half)).reshape(d, ATT_KV * LANES)
    wv = jnp.broadcast_to(w[:, v0:].reshape(d, ATT_KV, 1, ATT_HD),
                          (d, ATT_KV, 2, ATT_HD)).reshape(d, ATT_KV * LANES)
    return jnp.concatenate([w[:, :g0], gates, wq, wk, wv], axis=1).astype(BF16)


def _sigmoid(v):
    return 1.0 / (1.0 + jnp.exp(-v))


def _dot(a, b):
    return jnp.dot(a, b, preferred_element_type=F32)


def _dot_nt(a, b):
    return lax.dot_general(a, b, (((1,), (1,)), ((), ())), preferred_element_type=F32)


def _mod_row(mod_ref, k):
    return mod_ref[pl.ds(pl.program_id(0), 1), k * D_MODEL:(k + 1) * D_MODEL]


def _rms(v):
    return v * lax.rsqrt(jnp.mean(v * v, axis=-1, keepdims=True) + RMS_EPS)


def _adaln_kernel(c_ref, w_ref, b_ref, o_ref):
    c = c_ref[...]
    ca = (c * _sigmoid(c)).astype(BF16)
    o_ref[...] = _dot(ca, w_ref[...].astype(BF16)) + b_ref[...]


def _adaln(c, w, b):
    bsz, d = c.shape
    n = w.shape[1]
    tn = ADALN_TILE
    return pl.pallas_call(
        _adaln_kernel,
        grid=(n // tn,),
        in_specs=[
            pl.BlockSpec((bsz, d), lambda j: (0, 0)),
            pl.BlockSpec((d, tn), lambda j: (0, j)),
            pl.BlockSpec((1, tn), lambda j: (0, j)),
        ],
        out_specs=pl.BlockSpec((bsz, tn), lambda j: (0, j)),
        out_shape=jax.ShapeDtypeStruct((bsz, n), F32),
        compiler_params=pltpu.CompilerParams(dimension_semantics=("arbitrary",)),
        name="adaln",
    )(c, w, b.reshape(1, n))


def _dup(v):
    return jnp.concatenate([v, v], axis=1)


def _mlstm_tile(q_s, kt_ml_s, v_s, og_s, bc_s, ipt_s, bt_s, mlg_ref, mix_s, st_s, m_s, nblk, tril, ones_b):
    for r0 in range(0, nblk, MLSTM_BLOCKS_PER_STAGE):
        units = [(r, j) for r in range(r0, min(r0 + MLSTM_BLOCKS_PER_STAGE, nblk)) for j in range(ML_H)]
        rows = {u: slice(u[0] * BLK, (u[0] + 1) * BLK) for u in units}
        hcol = {u: slice(u[1] * ML_HD, (u[1] + 1) * ML_HD) for u in units}
        a_row, b_last, g_row, gmax = {}, {}, {}, {}
        for u in units:
            r, j = u
            i_row = ipt_s[r, j:j + 1, :]
            b_row = bt_s[r, ML_H + j:ML_H + j + 1, :]
            a_row[u] = i_row - b_row
            b_last[u] = b_row[:, BLK - 1:BLK]
            g_row[u] = b_last[u] - b_row + i_row
            gmax[u] = jnp.max(g_row[u], axis=-1, keepdims=True)
        dma = {u: jnp.where(tril, a_row[u], NEG) for u in units}
        cm_col = {u: jnp.max(dma[u], axis=-1, keepdims=True) for u in units}
        cm = {u: jnp.broadcast_to(cm_col[u], (BLK, LANES)) for u in units}
        pw = {u: jnp.exp2(dma[u] - cm[u]) for u in units}
        qk = {u: _dot(q_s[rows[u], hcol[u]], kt_ml_s[u[0], u[1]]) for u in units}
        nloc, aloc = {}, {}
        for u in units:
            v_aug = jnp.concatenate([v_s[rows[u], hcol[u]], ones_b], axis=1)
            nloc[u] = _dot((qk[u] * pw[u]).astype(BF16), v_aug)
        for u in units:
            v_aug = jnp.concatenate([v_s[rows[u], hcol[u]], ones_b], axis=1)
            kw = (kt_ml_s[u[0], u[1]].astype(F32) * jnp.exp2(g_row[u] - gmax[u])).astype(BF16)
            aloc[u] = _dot(kw, v_aug)
        m_prev, st_b = {}, {}
        for j in range(ML_H):
            st = st_s[j]
            m = m_s[j:j + 1, :]
            for r in range(r0, min(r0 + MLSTM_BLOCKS_PER_STAGE, nblk)):
                u = (r, j)
                m_prev[u] = m
                st_b[u] = st.astype(BF16)
                m_new = jnp.maximum(b_last[u] + m, gmax[u])
                decay = jnp.exp2(b_last[u] + m - m_new)
                gain = jnp.exp2(gmax[u] - m_new)
                st = _dup(decay) * st + _dup(gain) * aloc[u]
                m = m_new
            st_s[j] = st
            m_s[j:j + 1, :] = m
        inter = {u: _dot(q_s[rows[u], hcol[u]], st_b[u]) for u in units}
        hv, mm = {}, {}
        for u in units:
            mm[u] = jnp.maximum(m_prev[u], cm[u])
            num = nloc[u] * _dup(jnp.exp2(cm[u] - mm[u])) + _dup(jnp.exp2(m_prev[u] - mm[u])) * inter[u]
            b_rep = jnp.broadcast_to(bc_s[u[0], :, ML_H + u[1]:ML_H + u[1] + 1], (BLK, LANES))
            floor = jnp.exp2(-(b_rep + mm[u]))
            hv[u] = num[:, 0:ML_HD] / jnp.maximum(jnp.abs(num[:, ML_HD:2 * ML_HD]), floor)
        ms = {u: _dot((hv[u] * hv[u]).astype(BF16), ones_b) * (1.0 / ML_HD) for u in units}
        for u in units:
            hn = hv[u] * lax.rsqrt(ms[u] + RMS_EPS) * mlg_ref[:, hcol[u]]
            mix_s[rows[u], hcol[u]] = (hn * _sigmoid(og_s[rows[u], hcol[u]])).astype(BF16)


def _attention_tile(qa_s, kt_s, vt_s, sink_ref, atg_ref, mix_s, nblk, has_prev_tile, lane, ones_b):
    head_a = (lane % (LANES // 2)) < (LANES // 4)
    lane_lo = lane < (LANES // 2)
    rows4 = lax.broadcasted_iota(jnp.int32, (4 * BLK, BLK), 0) % BLK
    cols4 = lax.broadcasted_iota(jnp.int32, (4 * BLK, BLK), 1)
    from_prev = cols4 > rows4
    prev_bias = jnp.where(has_prev_tile, 0.0, NEG)
    sink4 = {}
    for g in range(ATT_KV):
        sink4[g] = jnp.concatenate(
            [jnp.broadcast_to(sink_ref[h:h + 1, :], (BLK, LANES)) for h in range(4 * g, 4 * g + 4)], axis=0)
    for r0 in range(0, nblk, ATT_BLOCKS_PER_STAGE):
        blocks = range(r0, min(r0 + ATT_BLOCKS_PER_STAGE, nblk))
        units = [(r, g) for r in blocks for g in range(ATT_KV)]
        sc = {}
        for u in units:
            r, g = u
            rows = slice(r * BLK, (r + 1) * BLK)
            qs = []
            for p in (2 * g, 2 * g + 1):
                qp = qa_s[rows, p * LANES:(p + 1) * LANES]
                qs.append(jnp.where(head_a, qp, jnp.zeros_like(qp)))
                qs.append(jnp.where(head_a, jnp.zeros_like(qp), qp))
            raw = _dot_nt(jnp.concatenate(qs, axis=0), kt_s[g, r * BLK:(r + 2) * BLK, :])
            raw_prev = raw[:, 0:BLK] + prev_bias if r == 0 else raw[:, 0:BLK]
            sc[u] = jnp.where(from_prev, raw_prev, raw[:, BLK:2 * BLK])
        m_col = {u: jnp.max(sc[u], axis=-1, keepdims=True) for u in units}
        m = {u: jnp.maximum(jnp.broadcast_to(m_col[u], (4 * BLK, LANES)), sink4[u[1]]) for u in units}
        pe = {u: jnp.exp2(sc[u] - m[u]).astype(BF16) for u in units}
        o4a = {}
        for u in units:
            zero = jnp.zeros_like(pe[u])
            banded = jnp.concatenate([jnp.where(from_prev, pe[u], zero), jnp.where(from_prev, zero, pe[u])], axis=1)
            o4a[u] = _dot(banded, vt_s[u[1], u[0] * BLK:(u[0] + 2) * BLK, :])
        for r in blocks:
            rows = slice(r * BLK, (r + 1) * BLK)
            pair_out = []
            for g in range(ATT_KV):
                u = (r, g)
                o4 = o4a[u][:, 0:LANES] / (o4a[u][:, LANES:2 * LANES] + jnp.exp2(sink4[g] - m[u]))
                for pp in range(2):
                    pair_out.append(jnp.where(lane_lo, o4[(2 * pp) * BLK:(2 * pp + 1) * BLK, :],
                                              o4[(2 * pp + 1) * BLK:(2 * pp + 2) * BLK, :]))
            sq = sum(po * po for po in pair_out)
            rs = lax.rsqrt(_dot(sq.astype(BF16), ones_b) * (1.0 / ATT_W) + RMS_EPS)
            for p in range(ATT_H // 2):
                mix_s[rows, ML_W + p * LANES:ML_W + (p + 1) * LANES] = (
                    pair_out[p] * rs * atg_ref[:, p * LANES:(p + 1) * LANES]).astype(BF16)


def _mixer_kernel(x_ref, mod_ref, gpre_ref, win_ref, cw_ref, cb_ref, gb_ref, mlg_ref, sink_ref,
                  atg_ref, cos_ref, sin_ref, wout_ref, gpost_ref, o_ref,
                  h_s, xp_s, q_s, kt_ml_s, v_s, og_s, bc_s, ipt_s, bt_s, qa_s, kt_s, vt_s,
                  mix_s, st_s, m_s):
    s_idx = pl.program_id(1)
    ts = x_ref.shape[0]
    nblk = ts // BLK

    @pl.when(s_idx == 0)
    def _reset_state():
        xp_s[...] = jnp.zeros(xp_s.shape, F32)
        kt_s[:, 0:BLK, :] = jnp.zeros((ATT_KV, BLK, LANES), BF16)
        vt_s[:, :, 0:LANES] = jnp.zeros((ATT_KV, BLK + ts, LANES), BF16)
        vt_s[:, :, LANES:2 * LANES] = jnp.ones((ATT_KV, BLK + ts, LANES), BF16)
        st_s[...] = jnp.zeros(st_s.shape, F32)
        m_s[...] = jnp.zeros(m_s.shape, F32)

    @pl.when(s_idx > 0)
    def _carry_kv():
        for g in range(ATT_KV):
            kt_s[g, 0:BLK, :] = kt_s[g, ts:ts + BLK, :]
            vt_s[g, 0:BLK, 0:LANES] = vt_s[g, ts:ts + BLK, 0:LANES]

    x = x_ref[...]
    h = _rms(x) * (gpre_ref[...] * (1.0 + _mod_row(mod_ref, 1))) + _mod_row(mod_ref, 0)
    h_s[...] = h.astype(BF16)

    kscale = 1.0 / math.sqrt(ML_HD)
    cchunk = QK_CONV_COLS
    for cj in range(2 * ML_W // cchunk):
        cols = slice(cj * cchunk, (cj + 1) * cchunk)
        u = _dot(h_s[...], win_ref[:, C_QK + cj * cchunk:C_QK + (cj + 1) * cchunk])
        ext = jnp.concatenate([xp_s[:, cols], u], axis=0)
        w = cw_ref[:, cols]
        y = u * w[ML_CONV - 1:ML_CONV] + cb_ref[:, cols]
        for back in range(1, ML_CONV):
            y = y + pltpu.roll(ext, back, 0)[SUBLANES:, :] * w[ML_CONV - 1 - back:ML_CONV - back]
        a = y * _sigmoid(y)
        if cj * cchunk < ML_W:
            q_s[:, cols] = a.astype(BF16)
        else:
            ak = a * kscale
            for jj in range(cchunk // ML_HD):
                j = (cj * cchunk - ML_W) // ML_HD + jj
                for r in range(nblk):
                    kt_ml_s[r, j] = ak[r * BLK:(r + 1) * BLK, jj * ML_HD:(jj + 1) * ML_HD].T.astype(BF16)
        xp_s[:, cols] = u[ts - SUBLANES:ts, :]

    v_s[...] = _dot(h_s[...], win_ref[:, C_V:C_V + ML_W]).astype(BF16)
    og_s[...] = _dot(h_s[...], win_ref[:, C_O:C_O + ML_W])

    gts = _dot(h_s[...], win_ref[:, C_G:C_G + LANES]) + gb_ref[...]
    lsg = jnp.minimum(gts, 0.0) - jnp.log(1.0 + jnp.exp(-jnp.abs(gts)))
    row_i = lax.broadcasted_iota(jnp.int32, (BLK, BLK), 0)
    col_i = lax.broadcasted_iota(jnp.int32, (BLK, BLK), 1)
    tril = row_i >= col_i
    tril_b = jnp.where(tril, 1.0, 0.0).astype(BF16)
    for r in range(nblk):
        ipr = gts[r * BLK:(r + 1) * BLK, :]
        lsr = lsg[r * BLK:(r + 1) * BLK, :]
        hi = lsr.astype(BF16)
        r1 = lsr - hi.astype(F32)
        mid = r1.astype(BF16)
        lo = (r1 - mid.astype(F32)).astype(BF16)
        bc = _dot(tril_b, hi) + _dot(tril_b, mid) + _dot(tril_b, lo)
        bc2 = bc * LOG2E
        bc_s[r] = bc2
        ipt_s[r] = (ipr * LOG2E).T[0:SUBLANES, :]
        bt_s[r] = bc2.T[0:SUBLANES, :]

    cos = cos_ref[...]
    sin = sin_ref[...]
    qat = _dot(h_s[...], win_ref[:, C_QA:C_QA + ATT_W])
    qscale = LOG2E / math.sqrt(ATT_HD)
    for p in range(ATT_H // 2):
        qp = qat[:, p * LANES:(p + 1) * LANES]
        qr = (qp * cos + pltpu.roll(qp, LANES // 2, 1) * sin) * qscale
        qa_s[:, p * LANES:(p + 1) * LANES] = qr.astype(BF16)
    kat = _dot(h_s[...], win_ref[:, C_KA:C_KA + ATT_KV * LANES])
    vat = _dot(h_s[...], win_ref[:, C_VA:C_VA + ATT_KV * LANES])
    for g in range(ATT_KV):
        kp = kat[:, g * LANES:(g + 1) * LANES]
        kr = kp * cos + pltpu.roll(kp, LANES // 2, 1) * sin
        kt_s[g, BLK:BLK + ts, :] = kr.astype(BF16)
        vt_s[g, BLK:BLK + ts, 0:LANES] = vat[:, g * LANES:(g + 1) * LANES].astype(BF16)

    lane = lax.broadcasted_iota(jnp.int32, (BLK, LANES), 1)
    ones_b = jnp.ones((BLK, LANES), BF16)
    _mlstm_tile(q_s, kt_ml_s, v_s, og_s, bc_s, ipt_s, bt_s, mlg_ref, mix_s, st_s, m_s, nblk, tril, ones_b)
    _attention_tile(qa_s, kt_s, vt_s, sink_ref, atg_ref, mix_s, nblk, s_idx > 0, lane, ones_b)

    y = _dot(mix_s[...], wout_ref[...])
    o_ref[...] = x_ref[...] + _rms(y) * (_mod_row(mod_ref, 2) * gpost_ref[...])


def _const_spec(shape):
    nd = len(shape)
    return pl.BlockSpec(shape, lambda b, s: (0,) * nd, pipeline_mode=pl.Buffered(1))


def _mixer(x, mod, gpre, win, cw, cb, gb, mlg, sinks, atg, cos, sin, wout, gpost, ts):
    bsz, seq, d = x.shape
    nblk = ts // BLK
    in_specs = [
        pl.BlockSpec((None, ts, d), lambda b, s: (b, s, 0)),
        _const_spec((bsz, N_MOD * d)),
        _const_spec((1, d)),
        _const_spec((d, NW)),
        _const_spec((ML_CONV, 2 * ML_W)),
        _const_spec((1, 2 * ML_W)),
        _const_spec((1, LANES)),
        _const_spec((1, ML_W)),
        _const_spec((ATT_H, LANES)),
        _const_spec((1, ATT_W)),
        pl.BlockSpec((ts, LANES), lambda b, s: (s, 0)),
        pl.BlockSpec((ts, LANES), lambda b, s: (s, 0)),
        _const_spec((MIX_W, d)),
        _const_spec((1, d)),
    ]
    scratch = [
        pltpu.VMEM((ts, d), BF16),
        pltpu.VMEM((SUBLANES, 2 * ML_W), F32),
        pltpu.VMEM((ts, ML_W), BF16),
        pltpu.VMEM((nblk, ML_H, ML_HD, BLK), BF16),
        pltpu.VMEM((ts, ML_W), BF16),
        pltpu.VMEM((ts, ML_W), F32),
        pltpu.VMEM((nblk, BLK, LANES), F32),
        pltpu.VMEM((nblk, SUBLANES, BLK), F32),
        pltpu.VMEM((nblk, SUBLANES, BLK), F32),
        pltpu.VMEM((ts, ATT_W), BF16),
        pltpu.VMEM((ATT_KV, BLK + ts, LANES), BF16),
        pltpu.VMEM((ATT_KV, BLK + ts, 2 * LANES), BF16),
        pltpu.VMEM((ts, MIX_W), BF16),
        pltpu.VMEM((ML_H, ML_HD, 2 * ML_HD), F32),
        pltpu.VMEM((SUBLANES, LANES), F32),
    ]
    return pl.pallas_call(
        _mixer_kernel,
        grid=(bsz, seq // ts),
        in_specs=in_specs,
        out_specs=pl.BlockSpec((None, ts, d), lambda b, s: (b, s, 0)),
        out_shape=jax.ShapeDtypeStruct((bsz, seq, d), F32),
        scratch_shapes=scratch,
        compiler_params=pltpu.CompilerParams(
            dimension_semantics=("arbitrary", "arbitrary"), vmem_limit_bytes=VMEM_LIMIT),
        name="mixer",
    )(x, mod, gpre, win, cw, cb, gb, mlg, sinks, atg, cos, sin, wout, gpost)


def _ffn_kernel(x_ref, mod_ref, gpre_ref, wup_ref, cw_ref, cb_ref, wdn_ref, gpost_ref, o_ref,
                slab_s, h_s, ub_s, carry_s, act_s):
    s_idx = pl.program_id(1)
    ts = x_ref.shape[0]
    nv = ts // SUBLANES
    pitch = slab_s.shape[1] // SUBLANES
    nslab = slab_s.shape[0]

    @pl.when(s_idx == 0)
    def _reset_state():
        carry_s[...] = jnp.zeros(carry_s.shape, F32)

    x = x_ref[...]
    h = _rms(x) * (gpre_ref[...] * (1.0 + _mod_row(mod_ref, 4))) + _mod_row(mod_ref, 3)
    for c in range(nslab):
        for s in range(SUBLANES):
            slab_s[c, s * pitch:s * pitch + nv, :] = h[s * nv:(s + 1) * nv, c * LANES:(c + 1) * LANES]
    for c in range(nslab):
        col = jnp.concatenate([slab_s[c, pl.ds(v, SUBLANES, stride=pitch), :] for v in range(nv)], axis=0)
        h_s[:, c * LANES:(c + 1) * LANES] = col.astype(BF16)

    def up_proj(j):
        return tuple(_dot(h_s[...], wup_ref[:, half * D_FF + j * FF_TILE:half * D_FF + (j + 1) * FF_TILE])
                     for half in range(2))

    first_sublane = lax.broadcasted_iota(jnp.int32, (SUBLANES, FF_TILE), 0) == 0

    def causal_conv(u, j, half):
        cols = slice(half * D_FF + j * FF_TILE, half * D_FF + (j + 1) * FF_TILE)
        slot = 2 * (j % 2) + half
        prev = carry_s[j, half]
        for k in range(2):
            cur = u[ts - (2 - k) * SUBLANES:ts - (1 - k) * SUBLANES, :]
            ub_s[slot, k * SUBLANES:(k + 1) * SUBLANES, :] = jnp.where(
                first_sublane, pltpu.roll(prev[k * SUBLANES:(k + 1) * SUBLANES, :], 1, 0), pltpu.roll(cur, 1, 0))
        ub_s[slot, 2 * SUBLANES:2 * SUBLANES + ts, :] = u
        carry_s[j, half] = u[ts - 2 * SUBLANES:ts, :]
        w = cw_ref[:, cols]
        return (u * w[2:3] + ub_s[slot, SUBLANES:SUBLANES + ts, :] * w[1:2]
                + ub_s[slot, 0:ts, :] * w[0:1] + cb_ref[:, cols])

    c1 = math.sqrt(2.0 / math.pi)
    c3 = c1 * 0.044715
    for j in range(N_FF_TILES):
        ug, uv = up_proj(j)
        g = causal_conv(ug, j, 0)
        half_val = causal_conv(uv, j, 1)
        th = jnp.tanh(g * (c3 * (g * g) + c1))
        gv = g * half_val
        act_s[:, j * FF_TILE:(j + 1) * FF_TILE] = (gv * th + gv).astype(BF16)

    y2 = _dot(act_s[...], wdn_ref[...])
    delta = _rms(y2) * (_mod_row(mod_ref, 5) * gpost_ref[...])

    for c in range(nslab):
        for v in range(nv):
            slab_s[c, pl.ds(v, SUBLANES, stride=pitch), :] = delta[v * SUBLANES:(v + 1) * SUBLANES,
                                                                   c * LANES:(c + 1) * LANES]
    for c in range(nslab):
        for s in range(SUBLANES):
            rows = slice(s * nv, (s + 1) * nv)
            lanes = slice(c * LANES, (c + 1) * LANES)
            o_ref[rows, lanes] = x_ref[rows, lanes] + slab_s[c, s * pitch:s * pitch + nv, :]


def _ffn(x, mod, gpre, wup, cw, cb, wdn, gpost, ts):
    bsz, seq, d = x.shape
    in_specs = [
        pl.BlockSpec((None, ts, d), lambda b, s: (b, s, 0)),
        _const_spec((bsz, N_MOD * d)),
        _const_spec((1, d)),
        _const_spec((d, 2 * D_FF)),
        _const_spec((FFN_CONV, 2 * D_FF)),
        _const_spec((1, 2 * D_FF)),
        _const_spec((D_FF, d)),
        _const_spec((1, d)),
    ]
    scratch = [
        pltpu.VMEM((d // LANES, ts + SUBLANES * SLAB_PAD_ROWS, LANES), F32),
        pltpu.VMEM((ts, d), BF16),
        pltpu.VMEM((4, 2 * SUBLANES + ts, FF_TILE), F32),
        pltpu.VMEM((N_FF_TILES, 2, 2 * SUBLANES, FF_TILE), F32),
        pltpu.VMEM((ts, D_FF), BF16),
    ]
    return pl.pallas_call(
        _ffn_kernel,
        grid=(bsz, seq // ts),
        in_specs=in_specs,
        out_specs=pl.BlockSpec((None, ts, d), lambda b, s: (b, s, 0)),
        out_shape=jax.ShapeDtypeStruct((bsz, seq, d), F32),
        scratch_shapes=scratch,
        compiler_params=pltpu.CompilerParams(
            dimension_semantics=("arbitrary", "arbitrary"), vmem_limit_bytes=VMEM_LIMIT),
        name="conv_ffn",
    )(x, mod, gpre, wup, cw, cb, wdn, gpost)


def _rope_tables(seq):
    half = ATT_HD // 2
    inv = ROPE_THETA ** (-jnp.arange(half, dtype=F32) / half)
    ang = jnp.arange(seq).astype(F32)[:, None] * inv[None, :]
    cos = jnp.cos(ang)
    sin = jnp.sin(ang)
    return jnp.tile(cos, (1, 4)), jnp.concatenate([-sin, -sin, sin, sin], axis=1)


def kernel(x, c, w_ada, b_ada, pre_mix_g, w_in, ml_conv_w, ml_conv_b, ml_i_b, ml_f_b, ml_norm_g,
           attn_sinks, attn_norm_g, w_out, post_mix_g, pre_ffn_g, w_up, ffn_conv_w, ffn_conv_b,
           w_down, post_ffn_g):
    bsz, seq, d = x.shape
    depth = w_ada.shape[0]
    ts = min(SEQ_TILE, seq)
    assert seq % ts == 0 and ts % BLK == 0 and d == D_MODEL
    cos, sin = _rope_tables(seq)
    for l in range(depth):
        mod = _adaln(c, w_ada[l], b_ada[l])
        win = _relayout_in_proj(w_in[l])
        gb = jnp.concatenate([ml_i_b[l], ml_f_b[l], jnp.zeros((LANES - 2 * ML_H,), F32)]).reshape(1, LANES)
        sinks = jnp.broadcast_to((attn_sinks[l] * LOG2E)[:, None], (ATT_H, LANES))
        x = _mixer(x, mod, pre_mix_g[l].reshape(1, d), win, ml_conv_w[l], ml_conv_b[l].reshape(1, -1),
                   gb, ml_norm_g[l].reshape(1, ML_W), sinks, attn_norm_g[l].reshape(1, ATT_W),
                   cos, sin, w_out[l].astype(BF16), post_mix_g[l].reshape(1, d), ts)
        half_value = jnp.concatenate([jnp.ones((D_FF,), F32), jnp.full((D_FF,), 0.5, F32)])
        x = _ffn(x, mod, pre_ffn_g[l].reshape(1, d), w_up[l].astype(BF16),
                 ffn_conv_w[l] * half_value, (ffn_conv_b[l] * half_value).reshape(1, -1),
                 w_down[l].astype(BF16), post_ffn_g[l].reshape(1, d), ts)
    return x
```

```python
import math

import jax
import jax.numpy as jnp
from jax import lax
from jax.experimental import pallas as pl
from jax.experimental.pallas import tpu as pltpu

F32 = jnp.float32
BF16 = jnp.bfloat16

D_MODEL = 1024
ML_H = 4
ML_HD = 128
ML_W = ML_H * ML_HD
ML_CONV = 4
ATT_H = 8
ATT_KV = 2
ATT_HD = 64
ATT_W = ATT_H * ATT_HD
ROPE_THETA = 10000.0
MIX_W = ML_W + ATT_W
D_FF = 2816
FFN_CONV = 3
N_MOD = 6
RMS_EPS = 1e-6

LANES = 128
SUBLANES = 8
BLK = 128
NEG = -1e30
LOG2E = math.log2(math.e)

C_QK = 0
C_V = 2 * ML_W
C_O = 3 * ML_W
C_G = 4 * ML_W
C_QA = C_G + LANES
C_KA = C_QA + ATT_W
C_VA = C_KA + ATT_KV * LANES
NW = C_VA + ATT_KV * LANES

FF_TILE = 256
N_FF_TILES = D_FF // FF_TILE
QK_CONV_COLS = 256
ADALN_TILE = 1024
SEQ_TILE = 1024

VMEM_LIMIT = 56 * 1024 * 1024

SLAB_PAD_ROWS = 8
MLSTM_BLOCKS_PER_STAGE = 2
ATT_BLOCKS_PER_STAGE = 8


def _relayout_in_proj(w):
    d = w.shape[0]
    half = ATT_HD // 2
    g0 = 4 * ML_W
    q0 = g0 + 2 * ML_H
    k0 = q0 + ATT_W
    v0 = k0 + ATT_KV * ATT_HD
    gates = jnp.pad(w[:, g0:q0], ((0, 0), (0, LANES - 2 * ML_H)))
    wq = w[:, q0:k0].reshape(d, ATT_H // 2, 2, 2, half).transpose(0, 1, 3, 2, 4).reshape(d, ATT_W)
    wk = jnp.broadcast_to(w[:, k0:v0].reshape(d, ATT_KV, 2, 1, half),
                          (d, ATT_KV, 2, 2, half)).reshape(d, ATT_KV * LANES)
    wv = jnp.broadcast_to(w[:, v0:].reshape(d, ATT_KV, 1, ATT_HD),
                          (d, ATT_KV, 2, ATT_HD)).reshape(d, ATT_KV * LANES)
    return jnp.concatenate([w[:, :g0], gates, wq, wk, wv], axis=1).astype(BF16)


def _sigmoid(v):
    return 1.0 / (1.0 + jnp.exp(-v))


def _dot(a, b):
    return jnp.dot(a, b, preferred_element_type=F32)


def _dot_nt(a, b):
    return lax.dot_general(a, b, (((1,), (1,)), ((), ())), preferred_element_type=F32)


def _mod_row(mod_ref, k):
    return mod_ref[pl.ds(pl.program_id(0), 1), k * D_MODEL:(k + 1) * D_MODEL]


def _rms(v):
    return v * lax.rsqrt(jnp.mean(v * v, axis=-1, keepdims=True) + RMS_EPS)


def _adaln_kernel(c_ref, w_ref, b_ref, o_ref):
    c = c_ref[...]
    ca = (c * _sigmoid(c)).astype(BF16)
    o_ref[...] = _dot(ca, w_ref[...].astype(BF16)) + b_ref[...]


def _adaln(c, w, b):
    bsz, d = c.shape
    n = w.shape[1]
    tn = ADALN_TILE
    return pl.pallas_call(
        _adaln_kernel,
        grid=(n // tn,),
        in_specs=[
            pl.BlockSpec((bsz, d), lambda j: (0, 0)),
            pl.BlockSpec((d, tn), lambda j: (0, j)),
            pl.BlockSpec((1, tn), lambda j: (0, j)),
        ],
        out_specs=pl.BlockSpec((bsz, tn), lambda j: (0, j)),
        out_shape=jax.ShapeDtypeStruct((bsz, n), F32),
        compiler_params=pltpu.CompilerParams(dimension_semantics=("arbitrary",)),
        name="adaln",
    )(c, w, b.reshape(1, n))


def _dup(v):
    return jnp.concatenate([v, v], axis=1)


def _mlstm_tile(q_s, kt_ml_s, v_s, og_s, bc_s, ipt_s, bt_s, mlg_ref, mix_s, st_s, m_s, nblk, tril, ones_b):
    for r0 in range(0, nblk, MLSTM_BLOCKS_PER_STAGE):
        units = [(r, j) for r in range(r0, min(r0 + MLSTM_BLOCKS_PER_STAGE, nblk)) for j in range(ML_H)]
        rows = {u: slice(u[0] * BLK, (u[0] + 1) * BLK) for u in units}
        hcol = {u: slice(u[1] * ML_HD, (u[1] + 1) * ML_HD) for u in units}
        a_row, b_last, g_row, gmax = {}, {}, {}, {}
        for u in units:
            r, j = u
            i_row = ipt_s[r, j:j + 1, :]
            b_row = bt_s[r, ML_H + j:ML_H + j + 1, :]
            a_row[u] = i_row - b_row
            b_last[u] = b_row[:, BLK - 1:BLK]
            g_row[u] = b_last[u] - b_row + i_row
            gmax[u] = jnp.max(g_row[u], axis=-1, keepdims=True)
        dma = {u: jnp.where(tril, a_row[u], NEG) for u in units}
        cm_col = {u: jnp.max(dma[u], axis=-1, keepdims=True) for u in units}
        cm = {u: jnp.broadcast_to(cm_col[u], (BLK, LANES)) for u in units}
        pw = {u: jnp.exp2(dma[u] - cm[u]) for u in units}
        qk = {u: _dot(q_s[rows[u], hcol[u]], kt_ml_s[u[0], u[1]]) for u in units}
        nloc, aloc = {}, {}
        for u in units:
            v_aug = jnp.concatenate([v_s[rows[u], hcol[u]], ones_b], axis=1)
            nloc[u] = _dot((qk[u] * pw[u]).astype(BF16), v_aug)
        for u in units:
            v_aug = jnp.concatenate([v_s[rows[u], hcol[u]], ones_b], axis=1)
            kw = (kt_ml_s[u[0], u[1]].astype(F32) * jnp.exp2(g_row[u] - gmax[u])).astype(BF16)
            aloc[u] = _dot(kw, v_aug)
        m_prev, st_b = {}, {}
        for j in range(ML_H):
            st = st_s[j]
            m = m_s[j:j + 1, :]
            for r in range(r0, min(r0 + MLSTM_BLOCKS_PER_STAGE, nblk)):
                u = (r, j)
                m_prev[u] = m
                st_b[u] = st.astype(BF16)
                m_new = jnp.maximum(b_last[u] + m, gmax[u])
                decay = jnp.exp2(b_last[u] + m - m_new)
                gain = jnp.exp2(gmax[u] - m_new)
                st = _dup(decay) * st + _dup(gain) * aloc[u]
                m = m_new
            st_s[j] = st
            m_s[j:j + 1, :] = m
        inter = {u: _dot(q_s[rows[u], hcol[u]], st_b[u]) for u in units}
        hv, mm = {}, {}
        for u in units:
            mm[u] = jnp.maximum(m_prev[u], cm[u])
            num = nloc[u] * _dup(jnp.exp2(cm[u] - mm[u])) + _dup(jnp.exp2(m_prev[u] - mm[u])) * inter[u]
            b_rep = jnp.broadcast_to(bc_s[u[0], :, ML_H + u[1]:ML_H + u[1] + 1], (BLK, LANES))
            floor = jnp.exp2(-(b_rep + mm[u]))
            hv[u] = num[:, 0:ML_HD] / jnp.maximum(jnp.abs(num[:, ML_HD:2 * ML_HD]), floor)
        ms = {u: _dot((hv[u] * hv[u]).astype(BF16), ones_b) * (1.0 / ML_HD) for u in units}
        for u in units:
            hn = hv[u] * lax.rsqrt(ms[u] + RMS_EPS) * mlg_ref[:, hcol[u]]
            mix_s[rows[u], hcol[u]] = (hn * _sigmoid(og_s[rows[u], hcol[u]])).astype(BF16)


def _attention_tile(qa_s, kt_s, vt_s, sink_ref, atg_ref, mix_s, nblk, has_prev_tile, lane, ones_b):
    head_a = (lane % (LANES // 2)) < (LANES // 4)
    lane_lo = lane < (LANES // 2)
    rows4 = lax.broadcasted_iota(jnp.int32, (4 * BLK, BLK), 0) % BLK
    cols4 = lax.broadcasted_iota(jnp.int32, (4 * BLK, BLK), 1)
    from_prev = cols4 > rows4
    prev_bias = jnp.where(has_prev_tile, 0.0, NEG)
    sink4 = {}
    for g in range(ATT_KV):
        sink4[g] = jnp.concatenate(
            [jnp.broadcast_to(sink_ref[h:h + 1, :], (BLK, LANES)) for h in range(4 * g, 4 * g + 4)], axis=0)
    for r0 in range(0, nblk, ATT_BLOCKS_PER_STAGE):
        blocks = range(r0, min(r0 + ATT_BLOCKS_PER_STAGE, nblk))
        units = [(r, g) for r in blocks for g in range(ATT_KV)]
        sc = {}
        for u in units:
            r, g = u
            rows = slice(r * BLK, (r + 1) * BLK)
            qs = []
            for p in (2 * g, 2 * g + 1):
                qp = qa_s[rows, p * LANES:(p + 1) * LANES]
                qs.append(jnp.where(head_a, qp, jnp.zeros_like(qp)))
                qs.append(jnp.where(head_a, jnp.zeros_like(qp), qp))
            raw = _dot_nt(jnp.concatenate(qs, axis=0), kt_s[g, r * BLK:(r + 2) * BLK, :])
            raw_prev = raw[:, 0:BLK] + prev_bias if r == 0 else raw[:, 0:BLK]
            sc[u] = jnp.where(from_prev, raw_prev, raw[:, BLK:2 * BLK])
        m_col = {u: jnp.max(sc[u], axis=-1, keepdims=True) for u in units}
        m = {u: jnp.maximum(jnp.broadcast_to(m_col[u], (4 * BLK, LANES)), sink4[u[1]]) for u in units}
        pe = {u: jnp.exp2(sc[u] - m[u]).astype(BF16) for u in units}
        o4a = {}
        for u in units:
            zero = jnp.zeros_like(pe[u])
            banded = jnp.concatenate([jnp.where(from_prev, pe[u], zero), jnp.where(from_prev, zero, pe[u])], axis=1)
            o4a[u] = _dot(banded, vt_s[u[1], u[0] * BLK:(u[0] + 2) * BLK, :])
        for r in blocks:
            rows = slice(r * BLK, (r + 1) * BLK)
            pair_out = []
            for g in range(ATT_KV):
                u = (r, g)
                o4 = o4a[u][:, 0:LANES] / (o4a[u][:, LANES:2 * LANES] + jnp.exp2(sink4[g] - m[u]))
                for pp in range(2):
                    pair_out.append(jnp.where(lane_lo, o4[(2 * pp) * BLK:(2 * pp + 1) * BLK, :],
                                              o4[(2 * pp + 1) * BLK:(2 * pp + 2) * BLK, :]))
            sq = sum(po * po for po in pair_out)
            rs = lax.rsqrt(_dot(sq.astype(BF16), ones_b) * (1.0 / ATT_W) + RMS_EPS)
            for p in range(ATT_H // 2):
                mix_s[rows, ML_W + p * LANES:ML_W + (p + 1) * LANES] = (
                    pair_out[p] * rs * atg_ref[:, p * LANES:(p + 1) * LANES]).astype(BF16)


def _mixer_kernel(x_ref, mod_ref, gpre_ref, win_ref, cw_ref, cb_ref, gb_ref, mlg_ref, sink_ref,
                  atg_ref, cos_ref, sin_ref, wout_ref, gpost_ref, o_ref,
                  h_s, xp_s, q_s, kt_ml_s, v_s, og_s, bc_s, ipt_s, bt_s, qa_s, kt_s, vt_s,
                  mix_s, st_s, m_s):
    s_idx = pl.program_id(1)
    ts = x_ref.shape[0]
    nblk = ts // BLK

    @pl.when(s_idx == 0)
    def _reset_state():
        xp_s[...] = jnp.zeros(xp_s.shape, F32)
        kt_s[:, 0:BLK, :] = jnp.zeros((ATT_KV, BLK, LANES), BF16)
        vt_s[:, :, 0:LANES] = jnp.zeros((ATT_KV, BLK + ts, LANES), BF16)
        vt_s[:, :, LANES:2 * LANES] = jnp.ones((ATT_KV, BLK + ts, LANES), BF16)
        st_s[...] = jnp.zeros(st_s.shape, F32)
        m_s[...] = jnp.zeros(m_s.shape, F32)

    @pl.when(s_idx > 0)
    def _carry_kv():
        for g in range(ATT_KV):
            kt_s[g, 0:BLK, :] = kt_s[g, ts:ts + BLK, :]
            vt_s[g, 0:BLK, 0:LANES] = vt_s[g, ts:ts + BLK, 0:LANES]

    x = x_ref[...]
    h = _rms(x) * (gpre_ref[...] * (1.0 + _mod_row(mod_ref, 1))) + _mod_row(mod_ref, 0)
    h_s[...] = h.astype(BF16)

    kscale = 1.0 / math.sqrt(ML_HD)
    cchunk = QK_CONV_COLS
    for cj in range(2 * ML_W // cchunk):
        cols = slice(cj * cchunk, (cj + 1) * cchunk)
        u = _dot(h_s[...], win_ref[:, C_QK + cj * cchunk:C_QK + (cj + 1) * cchunk])
        ext = jnp.concatenate([xp_s[:, cols], u], axis=0)
        w = cw_ref[:, cols]
        y = u * w[ML_CONV - 1:ML_CONV] + cb_ref[:, cols]
        for back in range(1, ML_CONV):
            y = y + pltpu.roll(ext, back, 0)[SUBLANES:, :] * w[ML_CONV - 1 - back:ML_CONV - back]
        a = y * _sigmoid(y)
        if cj * cchunk < ML_W:
            q_s[:, cols] = a.astype(BF16)
        else:
            ak = a * kscale
            for jj in range(cchunk // ML_HD):
                j = (cj * cchunk - ML_W) // ML_HD + jj
                for r in range(nblk):
                    kt_ml_s[r, j] = ak[r * BLK:(r + 1) * BLK, jj * ML_HD:(jj + 1) * ML_HD].T.astype(BF16)
        xp_s[:, cols] = u[ts - SUBLANES:ts, :]

    v_s[...] = _dot(h_s[...], win_ref[:, C_V:C_V + ML_W]).astype(BF16)
    og_s[...] = _dot(h_s[...], win_ref[:, C_O:C_O + ML_W])

    gts = _dot(h_s[...], win_ref[:, C_G:C_G + LANES]) + gb_ref[...]
    lsg = jnp.minimum(gts, 0.0) - jnp.log(1.0 + jnp.exp(-jnp.abs(gts)))
    row_i = lax.broadcasted_iota(jnp.int32, (BLK, BLK), 0)
    col_i = lax.broadcasted_iota(jnp.int32, (BLK, BLK), 1)
    tril = row_i >= col_i
    tril_b = jnp.where(tril, 1.0, 0.0).astype(BF16)
    for r in range(nblk):
        ipr = gts[r * BLK:(r + 1) * BLK, :]
        lsr = lsg[r * BLK:(r + 1) * BLK, :]
        hi = lsr.astype(BF16)
        r1 = lsr - hi.astype(F32)
        mid = r1.astype(BF16)
        lo = (r1 - mid.astype(F32)).astype(BF16)
        bc = _dot(tril_b, hi) + _dot(tril_b, mid) + _dot(tril_b, lo)
        bc2 = bc * LOG2E
        bc_s[r] = bc2
        ipt_s[r] = (ipr * LOG2E).T[0:SUBLANES, :]
        bt_s[r] = bc2.T[0:SUBLANES, :]

    cos = cos_ref[...]
    sin = sin_ref[...]
    qat = _dot(h_s[...], win_ref[:, C_QA:C_QA + ATT_W])
    qscale = LOG2E / math.sqrt(ATT_HD)
    for p in range(ATT_H // 2):
        qp = qat[:, p * LANES:(p + 1) * LANES]
        qr = (qp * cos + pltpu.roll(qp, LANES // 2, 1) * sin) * qscale
        qa_s[:, p * LANES:(p + 1) * LANES] = qr.astype(BF16)
    kat = _dot(h_s[...], win_ref[:, C_KA:C_KA + ATT_KV * LANES])
    vat = _dot(h_s[...], win_ref[:, C_VA:C_VA + ATT_KV * LANES])
    for g in range(ATT_KV):
        kp = kat[:, g * LANES:(g + 1) * LANES]
        kr = kp * cos + pltpu.roll(kp, LANES // 2, 1) * sin
        kt_s[g, BLK:BLK + ts, :] = kr.astype(BF16)
        vt_s[g, BLK:BLK + ts, 0:LANES] = vat[:, g * LANES:(g + 1) * LANES].astype(BF16)

    lane = lax.broadcasted_iota(jnp.int32, (BLK, LANES), 1)
    ones_b = jnp.ones((BLK, LANES), BF16)
    _mlstm_tile(q_s, kt_ml_s, v_s, og_s, bc_s, ipt_s, bt_s, mlg_ref, mix_s, st_s, m_s, nblk, tril, ones_b)
    _attention_tile(qa_s, kt_s, vt_s, sink_ref, atg_ref, mix_s, nblk, s_idx > 0, lane, ones_b)

    y = _dot(mix_s[...], wout_ref[...])
    o_ref[...] = x_ref[...] + _rms(y) * (_mod_row(mod_ref, 2) * gpost_ref[...])


def _const_spec(shape):
    nd = len(shape)
    return pl.BlockSpec(shape, lambda b, s: (0,) * nd, pipeline_mode=pl.Buffered(1))


def _mixer(x, mod, gpre, win, cw, cb, gb, mlg, sinks, atg, cos, sin, wout, gpost, ts):
    bsz, seq, d = x.shape
    nblk = ts // BLK
    in_specs = [
        pl.BlockSpec((None, ts, d), lambda b, s: (b, s, 0)),
        _const_spec((bsz, N_MOD * d)),
        _const_spec((1, d)),
        _const_spec((d, NW)),
        _const_spec((ML_CONV, 2 * ML_W)),
        _const_spec((1, 2 * ML_W)),
        _const_spec((1, LANES)),
        _const_spec((1, ML_W)),
        _const_spec((ATT_H, LANES)),
        _const_spec((1, ATT_W)),
        pl.BlockSpec((ts, LANES), lambda b, s: (s, 0)),
        pl.BlockSpec((ts, LANES), lambda b, s: (s, 0)),
        _const_spec((MIX_W, d)),
        _const_spec((1, d)),
    ]
    scratch = [
        pltpu.VMEM((ts, d), BF16),
        pltpu.VMEM((SUBLANES, 2 * ML_W), F32),
        pltpu.VMEM((ts, ML_W), BF16),
        pltpu.VMEM((nblk, ML_H, ML_HD, BLK), BF16),
        pltpu.VMEM((ts, ML_W), BF16),
        pltpu.VMEM((ts, ML_W), F32),
        pltpu.VMEM((nblk, BLK, LANES), F32),
        pltpu.VMEM((nblk, SUBLANES, BLK), F32),
        pltpu.VMEM((nblk, SUBLANES, BLK), F32),
        pltpu.VMEM((ts, ATT_W), BF16),
        pltpu.VMEM((ATT_KV, BLK + ts, LANES), BF16),
        pltpu.VMEM((ATT_KV, BLK + ts, 2 * LANES), BF16),
        pltpu.VMEM((ts, MIX_W), BF16),
        pltpu.VMEM((ML_H, ML_HD, 2 * ML_HD), F32),
        pltpu.VMEM((SUBLANES, LANES), F32),
    ]
    return pl.pallas_call(
        _mixer_kernel,
        grid=(bsz, seq // ts),
        in_specs=in_specs,
        out_specs=pl.BlockSpec((None, ts, d), lambda b, s: (b, s, 0)),
        out_shape=jax.ShapeDtypeStruct((bsz, seq, d), F32),
        scratch_shapes=scratch,
        compiler_params=pltpu.CompilerParams(
            dimension_semantics=("arbitrary", "arbitrary"), vmem_limit_bytes=VMEM_LIMIT),
        name="mixer",
    )(x, mod, gpre, win, cw, cb, gb, mlg, sinks, atg, cos, sin, wout, gpost)


def _ffn_kernel(x_ref, mod_ref, gpre_ref, wup_ref, cw_ref, cb_ref, wdn_ref, gpost_ref, o_ref,
                slab_s, h_s, ub_s, carry_s, act_s):
    s_idx = pl.program_id(1)
    ts = x_ref.shape[0]
    nv = ts // SUBLANES
    pitch = slab_s.shape[1] // SUBLANES
    nslab = slab_s.shape[0]

    @pl.when(s_idx == 0)
    def _reset_state():
        carry_s[...] = jnp.zeros(carry_s.shape, F32)

    x = x_ref[...]
    h = _rms(x) * (gpre_ref[...] * (1.0 + _mod_row(mod_ref, 4))) + _mod_row(mod_ref, 3)
    for c in range(nslab):
        for s in range(SUBLANES):
            slab_s[c, s * pitch:s * pitch + nv, :] = h[s * nv:(s + 1) * nv, c * LANES:(c + 1) * LANES]
    for c in range(nslab):
        col = jnp.concatenate([slab_s[c, pl.ds(v, SUBLANES, stride=pitch), :] for v in range(nv)], axis=0)
        h_s[:, c * LANES:(c + 1) * LANES] = col.astype(BF16)

    def up_proj(j):
        return tuple(_dot(h_s[...], wup_ref[:, half * D_FF + j * FF_TILE:half * D_FF + (j + 1) * FF_TILE])
                     for half in range(2))

    first_sublane = lax.broadcasted_iota(jnp.int32, (SUBLANES, FF_TILE), 0) == 0

    def causal_conv(u, j, half):
        cols = slice(half * D_FF + j * FF_TILE, half * D_FF + (j + 1) * FF_TILE)
        slot = 2 * (j % 2) + half
        prev = carry_s[j, half]
        for k in range(2):
            cur = u[ts - (2 - k) * SUBLANES:ts - (1 - k) * SUBLANES, :]
            ub_s[slot, k * SUBLANES:(k + 1) * SUBLANES, :] = jnp.where(
                first_sublane, pltpu.roll(prev[k * SUBLANES:(k + 1) * SUBLANES, :], 1, 0), pltpu.roll(cur, 1, 0))
        ub_s[slot, 2 * SUBLANES:2 * SUBLANES + ts, :] = u
        carry_s[j, half] = u[ts - 2 * SUBLANES:ts, :]
        w = cw_ref[:, cols]
        return (u * w[2:3] + ub_s[slot, SUBLANES:SUBLANES + ts, :] * w[1:2]
                + ub_s[slot, 0:ts, :] * w[0:1] + cb_ref[:, cols])

    c1 = math.sqrt(2.0 / math.pi)
    c3 = c1 * 0.044715
    for j in range(N_FF_TILES):
        ug, uv = up_proj(j)
        g = causal_conv(ug, j, 0)
        half_val = causal_conv(uv, j, 1)
        th = jnp.tanh(g * (c3 * (g * g) + c1))
        gv = g * half_val
        act_s[:, j * FF_TILE:(j + 1) * FF_TILE] = (gv * th + gv).astype(BF16)

    y2 = _dot(act_s[...], wdn_ref[...])
    delta = _rms(y2) * (_mod_row(mod_ref, 5) * gpost_ref[...])

    for c in range(nslab):
        for v in range(nv):
            slab_s[c, pl.ds(v, SUBLANES, stride=pitch), :] = delta[v * SUBLANES:(v + 1) * SUBLANES,
                                                                   c * LANES:(c + 1) * LANES]
    for c in range(nslab):
        for s in range(SUBLANES):
            rows = slice(s * nv, (s + 1) * nv)
            lanes = slice(c * LANES, (c + 1) * LANES)
            o_ref[rows, lanes] = x_ref[rows, lanes] + slab_s[c, s * pitch:s * pitch + nv, :]


def _ffn(x, mod, gpre, wup, cw, cb, wdn, gpost, ts):
    bsz, seq, d = x.shape
    in_specs = [
        pl.BlockSpec((None, ts, d), lambda b, s: (b, s, 0)),
        _const_spec((bsz, N_MOD * d)),
        _const_spec((1, d)),
        _const_spec((d, 2 * D_FF)),
        _const_spec((FFN_CONV, 2 * D_FF)),
        _const_spec((1, 2 * D_FF)),
        _const_spec((D_FF, d)),
        _const_spec((1, d)),
    ]
    scratch = [
        pltpu.VMEM((d // LANES, ts + SUBLANES * SLAB_PAD_ROWS, LANES), F32),
        pltpu.VMEM((ts, d), BF16),
        pltpu.VMEM((4, 2 * SUBLANES + ts, FF_TILE), F32),
        pltpu.VMEM((N_FF_TILES, 2, 2 * SUBLANES, FF_TILE), F32),
        pltpu.VMEM((ts, D_FF), BF16),
    ]
    return pl.pallas_call(
        _ffn_kernel,
        grid=(bsz, seq // ts),
        in_specs=in_specs,
        out_specs=pl.BlockSpec((None, ts, d), lambda b, s: (b, s, 0)),
        out_shape=jax.ShapeDtypeStruct((bsz, seq, d), F32),
        scratch_shapes=scratch,
        compiler_params=pltpu.CompilerParams(
            dimension_semantics=("arbitrary", "arbitrary"), vmem_limit_bytes=VMEM_LIMIT),
        name="conv_ffn",
    )(x, mod, gpre, wup, cw, cb, wdn, gpost)


def _rope_tables(seq):
    half = ATT_HD // 2
    inv = ROPE_THETA ** (-jnp.arange(half, dtype=F32) / half)
    ang = jnp.arange(seq).astype(F32)[:, None] * inv[None, :]
    cos = jnp.cos(ang)
    sin = jnp.sin(ang)
    return jnp.tile(cos, (1, 4)), jnp.concatenate([-sin, -sin, sin, sin], axis=1)


def kernel(x, c, w_ada, b_ada, pre_mix_g, w_in, ml_conv_w, ml_conv_b, ml_i_b, ml_f_b, ml_norm_g,
           attn_sinks, attn_norm_g, w_out, post_mix_g, pre_ffn_g, w_up, ffn_conv_w, ffn_conv_b,
           w_down, post_ffn_g):
    bsz, seq, d = x.shape
    depth = w_ada.shape[0]
    ts = min(SEQ_TILE, seq)
    assert seq % ts == 0 and ts % BLK == 0 and d == D_MODEL
    cos, sin = _rope_tables(seq)
    for l in range(depth):
        mod = _adaln(c, w_ada[l], b_ada[l])
        win = _relayout_in_proj(w_in[l])
        gb = jnp.concatenate([ml_i_b[l], ml_f_b[l], jnp.zeros((LANES - 2 * ML_H,), F32)]).reshape(1, LANES)
        sinks = jnp.broadcast_to((attn_sinks[l] * LOG2E)[:, None], (ATT_H, LANES))
        x = _mixer(x, mod, pre_mix_g[l].reshape(1, d), win, ml_conv_w[l], ml_conv_b[l].reshape(1, -1),
                   gb, ml_norm_g[l].reshape(1, ML_W), sinks, attn_norm_g[l].reshape(1, ATT_W),
                   cos, sin, w_out[l].astype(BF16), post_mix_g[l].reshape(1, d), ts)
        half_value = jnp.concatenate([jnp.ones((D_FF,), F32), jnp.full((D_FF,), 0.5, F32)])
        x = _ffn(x, mod, pre_ffn_g[l].reshape(1, d), w_up[l].astype(BF16),
                 ffn_conv_w[l] * half_value, (ffn_conv_b[l] * half_value).reshape(1, -1),
                 w_down[l].astype(BF16), post_ffn_g[l].reshape(1, d), ts)
    return x
```

```python
import math

import jax
import jax.numpy as jnp
from jax import lax
from jax.experimental import pallas as pl
from jax.experimental.pallas import tpu as pltpu

F32 = jnp.float32
BF16 = jnp.bfloat16

D_MODEL = 1024
ML_H = 4
ML_HD = 128
ML_W = ML_H * ML_HD
ML_CONV = 4
ATT_H = 8
ATT_KV = 2
ATT_HD = 64
ATT_W = ATT_H * ATT_HD
ROPE_THETA = 10000.0
MIX_W = ML_W + ATT_W
D_FF = 2816
FFN_CONV = 3
N_MOD = 6
RMS_EPS = 1e-6

LANES = 128
SUBLANES = 8
BLK = 128
NEG = -1e30
LOG2E = math.log2(math.e)

C_QK = 0
C_V = 2 * ML_W
C_O = 3 * ML_W
C_G = 4 * ML_W
C_QA = C_G + LANES
C_KA = C_QA + ATT_W
C_VA = C_KA + ATT_KV * LANES
NW = C_VA + ATT_KV * LANES

FF_TILE = 256
N_FF_TILES = D_FF // FF_TILE
QK_CONV_COLS = 256
ADALN_TILE = 1024
SEQ_TILE = 1024

VMEM_LIMIT = 56 * 1024 * 1024

SLAB_PAD_ROWS = 8
MLSTM_BLOCKS_PER_STAGE = 2
ATT_BLOCKS_PER_STAGE = 8


def _relayout_in_proj(w):
    d = w.shape[0]
    half = ATT_HD // 2
    g0 = 4 * ML_W
    q0 = g0 + 2 * ML_H
    k0 = q0 + ATT_W
    v0 = k0 + ATT_KV * ATT_HD
    gates = jnp.pad(w[:, g0:q0], ((0, 0), (0, LANES - 2 * ML_H)))
    wq = w[:, q0:k0].reshape(d, ATT_H // 2, 2, 2, half).transpose(0, 1, 3, 2, 4).reshape(d, ATT_W)
    wk = jnp.broadcast_to(w[:, k0:v0].reshape(d, ATT_KV, 2, 1, half),
                          (d, ATT_KV, 2, 2, half)).reshape(d, ATT_KV * LANES)
    wv = jnp.broadcast_to(w[:, v0:].reshape(d, ATT_KV, 1, ATT_HD),
                          (d, ATT_KV, 2, ATT_HD)).reshape(d, ATT_KV * LANES)
    return jnp.concatenate([w[:, :g0], gates, wq, wk, wv], axis=1).astype(BF16)


def _sigmoid(v):
    return 1.0 / (1.0 + jnp.exp(-v))


def _dot(a, b):
    return jnp.dot(a, b, preferred_element_type=F32)


def _dot_nt(a, b):
    return lax.dot_general(a, b, (((1,), (1,)), ((), ())), preferred_element_type=F32)


def _mod_row(mod_ref, k):
    return mod_ref[pl.ds(pl.program_id(0), 1), k * D_MODEL:(k + 1) * D_MODEL]


def _rms(v):
    return v * lax.rsqrt(jnp.mean(v * v, axis=-1, keepdims=True) + RMS_EPS)


def _adaln_kernel(c_ref, w_ref, b_ref, o_ref):
    c = c_ref[...]
    ca = (c * _sigmoid(c)).astype(BF16)
    o_ref[...] = _dot(ca, w_ref[...].astype(BF16)) + b_ref[...]


def _adaln(c, w, b):
    bsz, d = c.shape
    n = w.shape[1]
    tn = ADALN_TILE
    return pl.pallas_call(
        _adaln_kernel,
        grid=(n // tn,),
        in_specs=[
            pl.BlockSpec((bsz, d), lambda j: (0, 0)),
            pl.BlockSpec((d, tn), lambda j: (0, j)),
            pl.BlockSpec((1, tn), lambda j: (0, j)),
        ],
        out_specs=pl.BlockSpec((bsz, tn), lambda j: (0, j)),
        out_shape=jax.ShapeDtypeStruct((bsz, n), F32),
        compiler_params=pltpu.CompilerParams(dimension_semantics=("arbitrary",)),
        name="adaln",
    )(c, w, b.reshape(1, n))


def _dup(v):
    return jnp.concatenate([v, v], axis=1)


def _mlstm_tile(q_s, kt_ml_s, v_s, og_s, bc_s, ipt_s, bt_s, mlg_ref, mix_s, st_s, m_s, nblk, tril, ones_b):
    for r0 in range(0, nblk, MLSTM_BLOCKS_PER_STAGE):
        units = [(r, j) for r in range(r0, min(r0 + MLSTM_BLOCKS_PER_STAGE, nblk)) for j in range(ML_H)]
        rows = {u: slice(u[0] * BLK, (u[0] + 1) * BLK) for u in units}
        hcol = {u: slice(u[1] * ML_HD, (u[1] + 1) * ML_HD) for u in units}
        a_row, b_last, g_row, gmax = {}, {}, {}, {}
        for u in units:
            r, j = u
            i_row = ipt_s[r, j:j + 1, :]
            b_row = bt_s[r, ML_H + j:ML_H + j + 1, :]
            a_row[u] = i_row - b_row
            b_last[u] = b_row[:, BLK - 1:BLK]
            g_row[u] = b_last[u] - b_row + i_row
            gmax[u] = jnp.max(g_row[u], axis=-1, keepdims=True)
        dma = {u: jnp.where(tril, a_row[u], NEG) for u in units}
        cm_col = {u: jnp.max(dma[u], axis=-1, keepdims=True) for u in units}
        cm = {u: jnp.broadcast_to(cm_col[u], (BLK, LANES)) for u in units}
        pw = {u: jnp.exp2(dma[u] - cm[u]) for u in units}
        qk = {u: _dot(q_s[rows[u], hcol[u]], kt_ml_s[u[0], u[1]]) for u in units}
        wloc = {u: qk[u] * pw[u] for u in units}
        aloc = {}
        for u in units:
            v_aug = jnp.concatenate([v_s[rows[u], hcol[u]], ones_b], axis=1)
            kw = (kt_ml_s[u[0], u[1]].astype(F32) * jnp.exp2(g_row[u] - gmax[u])).astype(BF16)
            aloc[u] = _dot(kw, v_aug)
        m_prev, st_b = {}, {}
        for j in range(ML_H):
            st = st_s[j]
            m = m_s[j:j + 1, :]
            for r in range(r0, min(r0 + MLSTM_BLOCKS_PER_STAGE, nblk)):
                u = (r, j)
                m_prev[u] = m
                st_b[u] = st.astype(BF16)
                m_new = jnp.maximum(b_last[u] + m, gmax[u])
                decay = jnp.exp2(b_last[u] + m - m_new)
                gain = jnp.exp2(gmax[u] - m_new)
                st = _dup(decay) * st + _dup(gain) * aloc[u]
                m = m_new
            st_s[j] = st
            m_s[j:j + 1, :] = m
        hv, mm = {}, {}
        for u in units:
            mm[u] = jnp.maximum(m_prev[u], cm[u])
            lhs = jnp.concatenate(
                [(wloc[u] * jnp.exp2(cm[u] - mm[u])).astype(BF16),
                 (q_s[rows[u], hcol[u]].astype(F32) * jnp.exp2(m_prev[u] - mm[u])).astype(BF16)], axis=1)
            v_aug = jnp.concatenate([v_s[rows[u], hcol[u]], ones_b], axis=1)
            num = _dot(lhs, jnp.concatenate([v_aug, st_b[u]], axis=0))
            b_rep = jnp.broadcast_to(bc_s[u[0], :, ML_H + u[1]:ML_H + u[1] + 1], (BLK, LANES))
            floor = jnp.exp2(-(b_rep + mm[u]))
            hv[u] = num[:, 0:ML_HD] / jnp.maximum(jnp.abs(num[:, ML_HD:2 * ML_HD]), floor)
        ms = {u: _dot((hv[u] * hv[u]).astype(BF16), ones_b) * (1.0 / ML_HD) for u in units}
        for u in units:
            hn = hv[u] * lax.rsqrt(ms[u] + RMS_EPS) * mlg_ref[:, hcol[u]]
            mix_s[rows[u], hcol[u]] = (hn * _sigmoid(og_s[rows[u], hcol[u]])).astype(BF16)


def _attention_tile(qa_s, kt_s, vt_s, sink_ref, atg_ref, mix_s, nblk, has_prev_tile, lane, ones_b):
    head_a = (lane % (LANES // 2)) < (LANES // 4)
    lane_lo = lane < (LANES // 2)
    rows4 = lax.broadcasted_iota(jnp.int32, (4 * BLK, BLK), 0) % BLK
    cols4 = lax.broadcasted_iota(jnp.int32, (4 * BLK, BLK), 1)
    from_prev = cols4 > rows4
    prev_bias = jnp.where(has_prev_tile, 0.0, NEG)
    sink4 = {}
    for g in range(ATT_KV):
        sink4[g] = jnp.concatenate(
            [jnp.broadcast_to(sink_ref[h:h + 1, :], (BLK, LANES)) for h in range(4 * g, 4 * g + 4)], axis=0)
    for r0 in range(0, nblk, ATT_BLOCKS_PER_STAGE):
        blocks = range(r0, min(r0 + ATT_BLOCKS_PER_STAGE, nblk))
        units = [(r, g) for r in blocks for g in range(ATT_KV)]
        sc = {}
        for u in units:
            r, g = u
            rows = slice(r * BLK, (r + 1) * BLK)
            qs = []
            for p in (2 * g, 2 * g + 1):
                qp = qa_s[rows, p * LANES:(p + 1) * LANES]
                qs.append(jnp.where(head_a, qp, jnp.zeros_like(qp)))
                qs.append(jnp.where(head_a, jnp.zeros_like(qp), qp))
            raw = _dot_nt(jnp.concatenate(qs, axis=0), kt_s[g, r * BLK:(r + 2) * BLK, :])
            raw_prev = raw[:, 0:BLK] + prev_bias if r == 0 else raw[:, 0:BLK]
            sc[u] = jnp.where(from_prev, raw_prev, raw[:, BLK:2 * BLK])
        m_col = {u: jnp.max(sc[u], axis=-1, keepdims=True) for u in units}
        m = {u: jnp.maximum(jnp.broadcast_to(m_col[u], (4 * BLK, LANES)), sink4[u[1]]) for u in units}
        pe = {u: jnp.exp2(sc[u] - m[u]).astype(BF16) for u in units}
        o4a = {}
        for u in units:
            zero = jnp.zeros_like(pe[u])
            banded = jnp.concatenate([jnp.where(from_prev, pe[u], zero), jnp.where(from_prev, zero, pe[u])], axis=1)
            o4a[u] = _dot(banded, vt_s[u[1], u[0] * BLK:(u[0] + 2) * BLK, :])
        for r in blocks:
            rows = slice(r * BLK, (r + 1) * BLK)
            pair_out = []
            for g in range(ATT_KV):
                u = (r, g)
                o4 = o4a[u][:, 0:LANES] / (o4a[u][:, LANES:2 * LANES] + jnp.exp2(sink4[g] - m[u]))
                for pp in range(2):
                    pair_out.append(jnp.where(lane_lo, o4[(2 * pp) * BLK:(2 * pp + 1) * BLK, :],
                                              o4[(2 * pp + 1) * BLK:(2 * pp + 2) * BLK, :]))
            sq = sum(po * po for po in pair_out)
            rs = lax.rsqrt(_dot(sq.astype(BF16), ones_b) * (1.0 / ATT_W) + RMS_EPS)
            for p in range(ATT_H // 2):
                mix_s[rows, ML_W + p * LANES:ML_W + (p + 1) * LANES] = (
                    pair_out[p] * rs * atg_ref[:, p * LANES:(p + 1) * LANES]).astype(BF16)


def _mixer_kernel(x_ref, mod_ref, gpre_ref, win_ref, cw_ref, cb_ref, gb_ref, mlg_ref, sink_ref,
                  atg_ref, cos_ref, sin_ref, wout_ref, gpost_ref, o_ref,
                  h_s, xp_s, q_s, kt_ml_s, v_s, og_s, bc_s, ipt_s, bt_s, qa_s, kt_s, vt_s,
                  mix_s, st_s, m_s):
    s_idx = pl.program_id(1)
    ts = x_ref.shape[0]
    nblk = ts // BLK

    @pl.when(s_idx == 0)
    def _reset_state():
        xp_s[...] = jnp.zeros(xp_s.shape, F32)
        kt_s[:, 0:BLK, :] = jnp.zeros((ATT_KV, BLK, LANES), BF16)
        vt_s[:, :, 0:LANES] = jnp.zeros((ATT_KV, BLK + ts, LANES), BF16)
        vt_s[:, :, LANES:2 * LANES] = jnp.ones((ATT_KV, BLK + ts, LANES), BF16)
        st_s[...] = jnp.zeros(st_s.shape, F32)
        m_s[...] = jnp.zeros(m_s.shape, F32)

    @pl.when(s_idx > 0)
    def _carry_kv():
        for g in range(ATT_KV):
            kt_s[g, 0:BLK, :] = kt_s[g, ts:ts + BLK, :]
            vt_s[g, 0:BLK, 0:LANES] = vt_s[g, ts:ts + BLK, 0:LANES]

    x = x_ref[...]
    h = _rms(x) * (gpre_ref[...] * (1.0 + _mod_row(mod_ref, 1))) + _mod_row(mod_ref, 0)
    h_s[...] = h.astype(BF16)

    kscale = 1.0 / math.sqrt(ML_HD)
    cchunk = QK_CONV_COLS
    for cj in range(2 * ML_W // cchunk):
        cols = slice(cj * cchunk, (cj + 1) * cchunk)
        u = _dot(h_s[...], win_ref[:, C_QK + cj * cchunk:C_QK + (cj + 1) * cchunk])
        ext = jnp.concatenate([xp_s[:, cols], u], axis=0)
        w = cw_ref[:, cols]
        y = u * w[ML_CONV - 1:ML_CONV] + cb_ref[:, cols]
        for back in range(1, ML_CONV):
            y = y + pltpu.roll(ext, back, 0)[SUBLANES:, :] * w[ML_CONV - 1 - back:ML_CONV - back]
        a = y * _sigmoid(y)
        if cj * cchunk < ML_W:
            q_s[:, cols] = a.astype(BF16)
        else:
            ak = a * kscale
            for jj in range(cchunk // ML_HD):
                j = (cj * cchunk - ML_W) // ML_HD + jj
                for r in range(nblk):
                    kt_ml_s[r, j] = ak[r * BLK:(r + 1) * BLK, jj * ML_HD:(jj + 1) * ML_HD].T.astype(BF16)
        xp_s[:, cols] = u[ts - SUBLANES:ts, :]

    v_s[...] = _dot(h_s[...], win_ref[:, C_V:C_V + ML_W]).astype(BF16)
    og_s[...] = _dot(h_s[...], win_ref[:, C_O:C_O + ML_W])

    gts = _dot(h_s[...], win_ref[:, C_G:C_G + LANES]) + gb_ref[...]
    lsg = jnp.minimum(gts, 0.0) - jnp.log(1.0 + jnp.exp(-jnp.abs(gts)))
    row_i = lax.broadcasted_iota(jnp.int32, (BLK, BLK), 0)
    col_i = lax.broadcasted_iota(jnp.int32, (BLK, BLK), 1)
    tril = row_i >= col_i
    tril_b = jnp.where(tril, 1.0, 0.0).astype(BF16)
    for r in range(nblk):
        ipr = gts[r * BLK:(r + 1) * BLK, :]
        lsr = lsg[r * BLK:(r + 1) * BLK, :]
        hi = lsr.astype(BF16)
        r1 = lsr - hi.astype(F32)
        mid = r1.astype(BF16)
        lo = (r1 - mid.astype(F32)).astype(BF16)
        bc = _dot(tril_b, hi) + _dot(tril_b, mid) + _dot(tril_b, lo)
        bc2 = bc * LOG2E
        bc_s[r] = bc2
        ipt_s[r] = (ipr * LOG2E).T[0:SUBLANES, :]
        bt_s[r] = bc2.T[0:SUBLANES, :]

    cos = cos_ref[...]
    sin = sin_ref[...]
    qat = _dot(h_s[...], win_ref[:, C_QA:C_QA + ATT_W])
    qscale = LOG2E / math.sqrt(ATT_HD)
    for p in range(ATT_H // 2):
        qp = qat[:, p * LANES:(p + 1) * LANES]
        qr = (qp * cos + pltpu.roll(qp, LANES // 2, 1) * sin) * qscale
        qa_s[:, p * LANES:(p + 1) * LANES] = qr.astype(BF16)
    kat = _dot(h_s[...], win_ref[:, C_KA:C_KA + ATT_KV * LANES])
    vat = _dot(h_s[...], win_ref[:, C_VA:C_VA + ATT_KV * LANES])
    for g in range(ATT_KV):
        kp = kat[:, g * LANES:(g + 1) * LANES]
        kr = kp * cos + pltpu.roll(kp, LANES // 2, 1) * sin
        kt_s[g, BLK:BLK + ts, :] = kr.astype(BF16)
        vt_s[g, BLK:BLK + ts, 0:LANES] = vat[:, g * LANES:(g + 1) * LANES].astype(BF16)

    lane = lax.broadcasted_iota(jnp.int32, (BLK, LANES), 1)
    ones_b = jnp.ones((BLK, LANES), BF16)
    _mlstm_tile(q_s, kt_ml_s, v_s, og_s, bc_s, ipt_s, bt_s, mlg_ref, mix_s, st_s, m_s, nblk, tril, ones_b)
    _attention_tile(qa_s, kt_s, vt_s, sink_ref, atg_ref, mix_s, nblk, s_idx > 0, lane, ones_b)

    y = _dot(mix_s[...], wout_ref[...])
    o_ref[...] = x_ref[...] + _rms(y) * (_mod_row(mod_ref, 2) * gpost_ref[...])


def _const_spec(shape):
    nd = len(shape)
    return pl.BlockSpec(shape, lambda b, s: (0,) * nd, pipeline_mode=pl.Buffered(1))


def _mixer(x, mod, gpre, win, cw, cb, gb, mlg, sinks, atg, cos, sin, wout, gpost, ts):
    bsz, seq, d = x.shape
    nblk = ts // BLK
    in_specs = [
        pl.BlockSpec((None, ts, d), lambda b, s: (b, s, 0)),
        _const_spec((bsz, N_MOD * d)),
        _const_spec((1, d)),
        _const_spec((d, NW)),
        _const_spec((ML_CONV, 2 * ML_W)),
        _const_spec((1, 2 * ML_W)),
        _const_spec((1, LANES)),
        _const_spec((1, ML_W)),
        _const_spec((ATT_H, LANES)),
        _const_spec((1, ATT_W)),
        pl.BlockSpec((ts, LANES), lambda b, s: (s, 0)),
        pl.BlockSpec((ts, LANES), lambda b, s: (s, 0)),
        _const_spec((MIX_W, d)),
        _const_spec((1, d)),
    ]
    scratch = [
        pltpu.VMEM((ts, d), BF16),
        pltpu.VMEM((SUBLANES, 2 * ML_W), F32),
        pltpu.VMEM((ts, ML_W), BF16),
        pltpu.VMEM((nblk, ML_H, ML_HD, BLK), BF16),
        pltpu.VMEM((ts, ML_W), BF16),
        pltpu.VMEM((ts, ML_W), F32),
        pltpu.VMEM((nblk, BLK, LANES), F32),
        pltpu.VMEM((nblk, SUBLANES, BLK), F32),
        pltpu.VMEM((nblk, SUBLANES, BLK), F32),
        pltpu.VMEM((ts, ATT_W), BF16),
        pltpu.VMEM((ATT_KV, BLK + ts, LANES), BF16),
        pltpu.VMEM((ATT_KV, BLK + ts, 2 * LANES), BF16),
        pltpu.VMEM((ts, MIX_W), BF16),
        pltpu.VMEM((ML_H, ML_HD, 2 * ML_HD), F32),
        pltpu.VMEM((SUBLANES, LANES), F32),
    ]
    return pl.pallas_call(
        _mixer_kernel,
        grid=(bsz, seq // ts),
        in_specs=in_specs,
        out_specs=pl.BlockSpec((None, ts, d), lambda b, s: (b, s, 0)),
        out_shape=jax.ShapeDtypeStruct((bsz, seq, d), F32),
        scratch_shapes=scratch,
        compiler_params=pltpu.CompilerParams(
            dimension_semantics=("arbitrary", "arbitrary"), vmem_limit_bytes=VMEM_LIMIT),
        name="mixer",
    )(x, mod, gpre, win, cw, cb, gb, mlg, sinks, atg, cos, sin, wout, gpost)


def _ffn_kernel(x_ref, mod_ref, gpre_ref, wup_ref, cw_ref, cb_ref, wdn_ref, gpost_ref, o_ref,
                slab_s, h_s, ub_s, carry_s, act_s):
    s_idx = pl.program_id(1)
    ts = x_ref.shape[0]
    nv = ts // SUBLANES
    pitch = slab_s.shape[1] // SUBLANES
    nslab = slab_s.shape[0]

    @pl.when(s_idx == 0)
    def _reset_state():
        carry_s[...] = jnp.zeros(carry_s.shape, F32)

    x = x_ref[...]
    h = _rms(x) * (gpre_ref[...] * (1.0 + _mod_row(mod_ref, 4))) + _mod_row(mod_ref, 3)
    for c in range(nslab):
        for s in range(SUBLANES):
            slab_s[c, s * pitch:s * pitch + nv, :] = h[s * nv:(s + 1) * nv, c * LANES:(c + 1) * LANES]
    for c in range(nslab):
        col = jnp.concatenate([slab_s[c, pl.ds(v, SUBLANES, stride=pitch), :] for v in range(nv)], axis=0)
        h_s[:, c * LANES:(c + 1) * LANES] = col.astype(BF16)

    def up_proj(j):
        return tuple(_dot(h_s[...], wup_ref[:, half * D_FF + j * FF_TILE:half * D_FF + (j + 1) * FF_TILE])
                     for half in range(2))

    first_sublane = lax.broadcasted_iota(jnp.int32, (SUBLANES, FF_TILE), 0) == 0

    def causal_conv(u, j, half):
        cols = slice(half * D_FF + j * FF_TILE, half * D_FF + (j + 1) * FF_TILE)
        slot = 2 * (j % 2) + half
        prev = carry_s[j, half]
        for k in range(2):
            cur = u[ts - (2 - k) * SUBLANES:ts - (1 - k) * SUBLANES, :]
            ub_s[slot, k * SUBLANES:(k + 1) * SUBLANES, :] = jnp.where(
                first_sublane, pltpu.roll(prev[k * SUBLANES:(k + 1) * SUBLANES, :], 1, 0), pltpu.roll(cur, 1, 0))
        ub_s[slot, 2 * SUBLANES:2 * SUBLANES + ts, :] = u
        carry_s[j, half] = u[ts - 2 * SUBLANES:ts, :]
        w = cw_ref[:, cols]
        return (u * w[2:3] + ub_s[slot, SUBLANES:SUBLANES + ts, :] * w[1:2]
                + ub_s[slot, 0:ts, :] * w[0:1] + cb_ref[:, cols])

    c1 = math.sqrt(2.0 / math.pi)
    c3 = c1 * 0.044715
    for j in range(N_FF_TILES):
        ug, uv = up_proj(j)
        g = causal_conv(ug, j, 0)
        half_val = causal_conv(uv, j, 1)
        th = jnp.tanh(g * (c3 * (g * g) + c1))
        gv = g * half_val
        act_s[:, j * FF_TILE:(j + 1) * FF_TILE] = (gv * th + gv).astype(BF16)

    y2 = _dot(act_s[...], wdn_ref[...])
    delta = _rms(y2) * (_mod_row(mod_ref, 5) * gpost_ref[...])

    for c in range(nslab):
        for v in range(nv):
            slab_s[c, pl.ds(v, SUBLANES, stride=pitch), :] = delta[v * SUBLANES:(v + 1) * SUBLANES,
                                                                   c * LANES:(c + 1) * LANES]
    for c in range(nslab):
        for s in range(SUBLANES):
            rows = slice(s * nv, (s + 1) * nv)
            lanes = slice(c * LANES, (c + 1) * LANES)
            o_ref[rows, lanes] = x_ref[rows, lanes] + slab_s[c, s * pitch:s * pitch + nv, :]


def _ffn(x, mod, gpre, wup, cw, cb, wdn, gpost, ts):
    bsz, seq, d = x.shape
    in_specs = [
        pl.BlockSpec((None, ts, d), lambda b, s: (b, s, 0)),
        _const_spec((bsz, N_MOD * d)),
        _const_spec((1, d)),
        _const_spec((d, 2 * D_FF)),
        _const_spec((FFN_CONV, 2 * D_FF)),
        _const_spec((1, 2 * D_FF)),
        _const_spec((D_FF, d)),
        _const_spec((1, d)),
    ]
    scratch = [
        pltpu.VMEM((d // LANES, ts + SUBLANES * SLAB_PAD_ROWS, LANES), F32),
        pltpu.VMEM((ts, d), BF16),
        pltpu.VMEM((4, 2 * SUBLANES + ts, FF_TILE), F32),
        pltpu.VMEM((N_FF_TILES, 2, 2 * SUBLANES, FF_TILE), F32),
        pltpu.VMEM((ts, D_FF), BF16),
    ]
    return pl.pallas_call(
        _ffn_kernel,
        grid=(bsz, seq // ts),
        in_specs=in_specs,
        out_specs=pl.BlockSpec((None, ts, d), lambda b, s: (b, s, 0)),
        out_shape=jax.ShapeDtypeStruct((bsz, seq, d), F32),
        scratch_shapes=scratch,
        compiler_params=pltpu.CompilerParams(
            dimension_semantics=("arbitrary", "arbitrary"), vmem_limit_bytes=VMEM_LIMIT),
        name="conv_ffn",
    )(x, mod, gpre, wup, cw, cb, wdn, gpost)


def _rope_tables(seq):
    half = ATT_HD // 2
    inv = ROPE_THETA ** (-jnp.arange(half, dtype=F32) / half)
    ang = jnp.arange(seq).astype(F32)[:, None] * inv[None, :]
    cos = jnp.cos(ang)
    sin = jnp.sin(ang)
    return jnp.tile(cos, (1, 4)), jnp.concatenate([-sin, -sin, sin, sin], axis=1)


def kernel(x, c, w_ada, b_ada, pre_mix_g, w_in, ml_conv_w, ml_conv_b, ml_i_b, ml_f_b, ml_norm_g,
           attn_sinks, attn_norm_g, w_out, post_mix_g, pre_ffn_g, w_up, ffn_conv_w, ffn_conv_b,
           w_down, post_ffn_g):
    bsz, seq, d = x.shape
    depth = w_ada.shape[0]
    ts = min(SEQ_TILE, seq)
    assert seq % ts == 0 and ts % BLK == 0 and d == D_MODEL
    cos, sin = _rope_tables(seq)
    for l in range(depth):
        mod = _adaln(c, w_ada[l], b_ada[l])
        win = _relayout_in_proj(w_in[l])
        gb = jnp.concatenate([ml_i_b[l], ml_f_b[l], jnp.zeros((LANES - 2 * ML_H,), F32)]).reshape(1, LANES)
        sinks = jnp.broadcast_to((attn_sinks[l] * LOG2E)[:, None], (ATT_H, LANES))
        x = _mixer(x, mod, pre_mix_g[l].reshape(1, d), win, ml_conv_w[l], ml_conv_b[l].reshape(1, -1),
                   gb, ml_norm_g[l].reshape(1, ML_W), sinks, attn_norm_g[l].reshape(1, ATT_W),
                   cos, sin, w_out[l].astype(BF16), post_mix_g[l].reshape(1, d), ts)
        half_value = jnp.concatenate([jnp.ones((D_FF,), F32), jnp.full((D_FF,), 0.5, F32)])
        x = _ffn(x, mod, pre_ffn_g[l].reshape(1, d), w_up[l].astype(BF16),
                 ffn_conv_w[l] * half_value, (ffn_conv_b[l] * half_value).reshape(1, -1),
                 w_down[l].astype(BF16), post_ffn_g[l].reshape(1, d), ts)
    return x
```
